```python
import math
import jax, jax.numpy as jnp
from jax import lax
import numpy as np

D_MODEL = 1024
BATCH = 2
SEQ = 8192
DEPTH = 1

Q_BLOCK = 128
DA_HEADS = 4
DA_DIM = 64
NSA_HEADS = 8
NSA_GROUPS = 2
NSA_DIM = 64
NSA_CMP_BLOCK = 32
NSA_SEL_BLOCK = 64
NSA_TOPN = 16
NSA_WINDOW = 512
NSA_CMP_HIDDEN = 256
MEM_LEN = 256
CA_HEADS = 4
CA_DIM = 128
PEER_HEADS = 8
PEER_NKEYS = 128
PEER_EXPERTS = PEER_NKEYS * PEER_NKEYS
PEER_DKEY = 256
PEER_TOPK = 16
PEER_CHUNK = 128

DA_QK = DA_HEADS * 2 * DA_DIM
DA_V = DA_HEADS * 2 * DA_DIM
NSA_Q = NSA_HEADS * NSA_DIM
NSA_KV = 6 * NSA_GROUPS * NSA_DIM
NSA_G = NSA_HEADS * 3
IN_SIZES = (DA_QK, DA_QK, DA_V, NSA_Q, NSA_KV, NSA_G, D_MODEL, D_MODEL)
IN_TOTAL = DA_QK * 2 + DA_V + NSA_Q + NSA_KV + NSA_G + 2 * D_MODEL

kernel_name = "hybrid_diffattn_nsa_peer_block"


def rms_norm(x, g, eps=1e-6):
    xf = x.astype(jnp.float32)
    y = xf * lax.rsqrt(jnp.mean(xf * xf, axis=-1, keepdims=True) + eps)
    return (y * g.astype(jnp.float32)).astype(x.dtype)


def alibi_slopes(n_heads):
    return 2.0 ** (-8.0 * jnp.arange(1, n_heads + 1, dtype=jnp.float32) / n_heads)


def diff_attention(q, k, v, lam_params, subln_g, lambda_init):
    B, S, H, _, d = q.shape
    nqb = S // Q_BLOCK
    lam = (jnp.exp(jnp.sum(lam_params[0] * lam_params[1]).astype(jnp.float32))
           - jnp.exp(jnp.sum(lam_params[2] * lam_params[3]).astype(jnp.float32)) + lambda_init)
    slopes = alibi_slopes(H)
    scale = d ** -0.5
    kpos = jnp.arange(S)
    qb = q.reshape(B, nqb, Q_BLOCK, H, 2, d).transpose(1, 0, 2, 3, 4, 5)

    def block(args):
        qi, i = args
        qpos = i * Q_BLOCK + jnp.arange(Q_BLOCK)
        dist = (qpos[:, None] - kpos[None, :]).astype(jnp.float32)
        s = jnp.einsum('bqhcd,bkhcd->bhcqk', qi, k).astype(jnp.float32) * scale
        s = s - slopes[None, :, None, None, None] * dist
        s = jnp.where(dist >= 0, s, -jnp.inf)
        p = jax.nn.softmax(s, axis=-1)
        a = p[:, :, 0] - lam * p[:, :, 1]
        return jnp.einsum('bhqk,bkhe->bqhe', a.astype(v.dtype), v)

    o = lax.map(block, (qb, jnp.arange(nqb)))
    o = o.transpose(1, 0, 2, 3, 4).reshape(B, S, H, 2 * d)
    o = rms_norm(o, subln_g) * (1.0 - lambda_init)
    return o.reshape(B, S, H * 2 * d)


def nsa_attention(q, kv, gates, cmp_pos, ck_w1, ck_w2, cv_w1, cv_w2):
    B, S, G, R, dh = q.shape
    slopes = alibi_slopes(G * R).reshape(G, R)
    scale = dh ** -0.5
    pos = jnp.arange(S)
    k_c, v_c, k_s, v_s, k_w, v_w = [kv[:, :, j] for j in range(6)]

    nc = S // NSA_CMP_BLOCK

    def compress(t, w1, w2):
        tb = t.reshape(B, nc, NSA_CMP_BLOCK, G, dh) + cmp_pos[None, None, :, None, :]
        tb = tb.transpose(0, 1, 3, 2, 4).reshape(B, nc, G, NSA_CMP_BLOCK * dh)
        return jax.nn.gelu(tb @ w1, approximate=False) @ w2

    kc = compress(k_c, ck_w1, ck_w2)
    vc = compress(v_c, cv_w1, cv_w2)
    cpos = jnp.arange(nc) * NSA_CMP_BLOCK + (NSA_CMP_BLOCK - 1)
    dist_c = (pos[:, None] - cpos[None, :]).astype(jnp.float32)
    valid_c = dist_c >= 0
    s = jnp.einsum('bsgrd,bcgd->bgrsc', q, kc).astype(jnp.float32) * scale
    s = s - slopes[None, :, :, None, None] * dist_c
    s = jnp.where(valid_c, s, -1e30)
    p_cmp = jax.nn.softmax(s, axis=-1) * valid_c
    o_cmp = jnp.einsum('bgrsc,bcgd->bsgrd', p_cmp.astype(vc.dtype), vc)

    nsb = S // NSA_SEL_BLOCK
    ratio = NSA_SEL_BLOCK // NSA_CMP_BLOCK
    p_slc = p_cmp.sum(axis=2).reshape(B, G, S, nsb, ratio).sum(-1)
    blk = jnp.arange(nsb)
    cur = pos // NSA_SEL_BLOCK
    valid_b = blk[None, :] <= cur[:, None]
    forced = (blk[None, :] == 0) | (blk[None, :] == cur[:, None]) | (blk[None, :] == cur[:, None] - 1)
    score = jnp.where(forced, 1e4, jnp.where(valid_b, p_slc, -1e9))
    n_sel = min(NSA_TOPN, nsb)
    _, idx = lax.top_k(score, n_sel)

    nqb = S // Q_BLOCK
    ksb = k_s.reshape(B, nsb, NSA_SEL_BLOCK, G, dh).transpose(0, 3, 1, 2, 4)
    vsb = v_s.reshape(B, nsb, NSA_SEL_BLOCK, G, dh).transpose(0, 3, 1, 2, 4)
    q_blocks = q.reshape(B, nqb, Q_BLOCK, G, R, dh).transpose(1, 0, 2, 3, 4, 5)
    idx_blocks = idx.reshape(B, G, nqb, Q_BLOCK, n_sel).transpose(2, 0, 1, 3, 4)
    b_ix = jnp.arange(B)[:, None, None, None]
    g_ix = jnp.arange(G)[None, :, None, None]
    off = jnp.arange(NSA_SEL_BLOCK)

    def sel_block(args):
        qi, ii, i = args
        kg = ksb[b_ix, g_ix, ii]
        vg = vsb[b_ix, g_ix, ii]
        qpos = i * Q_BLOCK + jnp.arange(Q_BLOCK)
        kpos = ii[..., None] * NSA_SEL_BLOCK + off
        dist = (qpos[None, None, :, None, None] - kpos).astype(jnp.float32)[:, :, None]
        ss = jnp.einsum('bqgrd,bgqnld->bgrqnl', qi, kg).astype(jnp.float32) * scale
        ss = ss - slopes[None, :, :, None, None, None] * dist
        ss = jnp.where(dist >= 0, ss, -jnp.inf)
        pp = jax.nn.softmax(ss.reshape(B, G, R, Q_BLOCK, n_sel * NSA_SEL_BLOCK), axis=-1)
        pp = pp.reshape(B, G, R, Q_BLOCK, n_sel, NSA_SEL_BLOCK).astype(vg.dtype)
        return jnp.einsum('bgrqnl,bgqnld->bqgrd', pp, vg)

    o_slc = lax.map(sel_block, (q_blocks, idx_blocks, jnp.arange(nqb)))
    o_slc = o_slc.transpose(1, 0, 2, 3, 4, 5).reshape(B, S, G, R, dh)

    wb = NSA_WINDOW // Q_BLOCK
    pad = jnp.zeros((B, NSA_WINDOW, G, dh), k_w.dtype)
    kwp = jnp.concatenate([pad, k_w], axis=1).reshape(B, nqb + wb, Q_BLOCK, G, dh)
    vwp = jnp.concatenate([pad, v_w], axis=1).reshape(B, nqb + wb, Q_BLOCK, G, dh)
    kwin = jnp.concatenate([kwp[:, j:j + nqb] for j in range(wb + 1)], axis=2)
    vwin = jnp.concatenate([vwp[:, j:j + nqb] for j in range(wb + 1)], axis=2)
    qblk = q.reshape(B, nqb, Q_BLOCK, G, R, dh)
    qpos = jnp.arange(nqb)[:, None] * Q_BLOCK + jnp.arange(Q_BLOCK)
    kpos = jnp.arange(nqb)[:, None] * Q_BLOCK - NSA_WINDOW + jnp.arange((wb + 1) * Q_BLOCK)
    dist = (qpos[:, :, None] - kpos[:, None, :]).astype(jnp.float32)
    valid = (dist >= 0) & (dist < NSA_WINDOW) & (kpos[:, None, :] >= 0)
    sw = jnp.einsum('bnqgrd,bnkgd->bngrqk', qblk, kwin).astype(jnp.float32) * scale
    sw = sw - slopes[None, None, :, :, None, None] * dist[None, :, None, None]
    sw = jnp.where(valid[None, :, None, None], sw, -jnp.inf)
    pw = jax.nn.softmax(sw, axis=-1).astype(vwin.dtype)
    o_win = jnp.einsum('bngrqk,bnkgd->bnqgrd', pw, vwin).reshape(B, S, G, R, dh)

    o = gates[..., 0:1] * o_cmp + gates[..., 1:2] * o_slc + gates[..., 2:3] * o_win
    return o.reshape(B, S, G * R * dh)


def memory_cross_attention(u, m, wq, wkv, wo):
    B, S, _ = u.shape
    M = m.shape[1]
    q = (u @ wq).reshape(B, S, CA_HEADS, CA_DIM)
    kv = (m @ wkv).reshape(B, M, 2, CA_HEADS, CA_DIM)
    s = jnp.einsum('bshd,bmhd->bhsm', q, kv[:, :, 0]).astype(jnp.float32) * (CA_DIM ** -0.5)
    p = jax.nn.softmax(s, axis=-1).astype(u.dtype)
    o = jnp.einsum('bhsm,bmhd->bshd', p, kv[:, :, 1]).reshape(B, S, CA_HEADS * CA_DIM)
    return o @ wo


def peer_ffn(u, w_q, sub_keys, u_tab, v_tab):
    B, S, D = u.shape
    T = B * S
    K = PEER_TOPK
    H = PEER_HEADS
    xt = u.reshape(T, D)
    q = (xt @ w_q).reshape(T, H, 2, PEER_DKEY // 2)
    s = jnp.einsum('thcd,hcnd->thcn', q, sub_keys).astype(jnp.float32)
    top_s, top_i = lax.top_k(s, K)
    cand = (top_s[:, :, 0, :, None] + top_s[:, :, 1, None, :]).reshape(T, H, K * K)
    cand_i = (top_i[:, :, 0, :, None] * PEER_NKEYS + top_i[:, :, 1, None, :]).reshape(T, H, K * K)
    best_s, best_j = lax.top_k(cand, K)
    expert = jnp.take_along_axis(cand_i, best_j, axis=-1)
    g = jax.nn.softmax(best_s, axis=-1).astype(u.dtype)
    nch = T // PEER_CHUNK

    def chunk(args):
        xc, ec, gc = args
        act = jax.nn.gelu(jnp.einsum('cd,chkd->chk', xc, u_tab[ec]), approximate=False)
        return jnp.einsum('chk,chkd->cd', gc * act, v_tab[ec])

    out = lax.map(chunk, (xt.reshape(nch, PEER_CHUNK, D),
                          expert.reshape(nch, PEER_CHUNK, H, K),
                          g.reshape(nch, PEER_CHUNK, H, K)))
    return out.reshape(B, S, D)


def setup_inputs(seed: int = 0) -> dict:
    key = jax.random.key(seed)
    ks = jax.random.split(key, 32)
    L, D = DEPTH, D_MODEL

    def nrm(k, shape, scale):
        return jax.random.normal(k, shape, jnp.float32) * scale

    def gain(k, shape):
        return 1.0 + 0.02 * jax.random.normal(k, shape, jnp.float32)

    return {
        "x": nrm(ks[0], (BATCH, SEQ, D), 1.0),
        "mem": nrm(ks[1], (BATCH, MEM_LEN, D), 1.0),
        "norm_mix": gain(ks[2], (L, D)),
        "w_in": nrm(ks[3], (L, D, IN_TOTAL), D ** -0.5),
        "diff_lambda": nrm(ks[4], (L, 4, DA_DIM), 0.1),
        "diff_subln": gain(ks[5], (L, 2 * DA_DIM)),
        "nsa_cmp_pos": nrm(ks[6], (L, NSA_CMP_BLOCK, NSA_DIM), 0.1),
        "nsa_ck_w1": nrm(ks[7], (L, NSA_CMP_BLOCK * NSA_DIM, NSA_CMP_HIDDEN), (NSA_CMP_BLOCK * NSA_DIM) ** -0.5),
        "nsa_ck_w2": nrm(ks[8], (L, NSA_CMP_HIDDEN, NSA_DIM), NSA_CMP_HIDDEN ** -0.5),
        "nsa_cv_w1": nrm(ks[9], (L, NSA_CMP_BLOCK * NSA_DIM, NSA_CMP_HIDDEN), (NSA_CMP_BLOCK * NSA_DIM) ** -0.5),
        "nsa_cv_w2": nrm(ks[10], (L, NSA_CMP_HIDDEN, NSA_DIM), NSA_CMP_HIDDEN ** -0.5),
        "w_branch_a": nrm(ks[11], (L, DA_V, D), DA_V ** -0.5),
        "w_branch_b": nrm(ks[12], (L, NSA_Q, D), NSA_Q ** -0.5),
        "w_out": nrm(ks[13], (L, D, D), D ** -0.5),
        "norm_cross": gain(ks[14], (L, D)),
        "norm_mem": gain(ks[15], (L, D)),
        "w_cross_q": nrm(ks[16], (L, D, CA_HEADS * CA_DIM), D ** -0.5),
        "w_cross_kv": nrm(ks[17], (L, D, 2 * CA_HEADS * CA_DIM), D ** -0.5),
        "w_cross_o": nrm(ks[18], (L, CA_HEADS * CA_DIM, D), (CA_HEADS * CA_DIM) ** -0.5),
        "norm_ffn": gain(ks[19], (L, D)),
        "peer_wq": nrm(ks[20], (L, D, PEER_HEADS * PEER_DKEY), D ** -0.5),
        "peer_subkeys": nrm(ks[21], (L, PEER_HEADS, 2, PEER_NKEYS, PEER_DKEY // 2), (PEER_DKEY // 2) ** -0.5),
        "peer_u": nrm(ks[22], (L, PEER_EXPERTS, D), D ** -0.5),
        "peer_v": nrm(ks[23], (L, PEER_EXPERTS, D), (PEER_HEADS * PEER_TOPK) ** -0.5),
        "norm_final": gain(ks[24], (D,)),
    }


def reference(x, mem, norm_mix, w_in, diff_lambda, diff_subln, nsa_cmp_pos, nsa_ck_w1, nsa_ck_w2,
              nsa_cv_w1, nsa_cv_w2, w_branch_a, w_branch_b, w_out, norm_cross, norm_mem,
              w_cross_q, w_cross_kv, w_cross_o, norm_ffn, peer_wq, peer_subkeys, peer_u, peer_v,
              norm_final):
    B, S, D = x.shape
    split_at = np.cumsum(IN_SIZES)[:-1].tolist()
    h = x
    for l in range(DEPTH):
        lambda_init = 0.8 - 0.6 * math.exp(-0.3 * l)
        u = rms_norm(h, norm_mix[l])
        qa, ka, va, qb, kvb, gb, gate_a, gate_b = jnp.split(u @ w_in[l], split_at, axis=-1)
        o_a = diff_attention(qa.reshape(B, S, DA_HEADS, 2, DA_DIM),
                             ka.reshape(B, S, DA_HEADS, 2, DA_DIM),
                             va.reshape(B, S, DA_HEADS, 2 * DA_DIM),
                             diff_lambda[l], diff_subln[l], lambda_init)
        R = NSA_HEADS // NSA_GROUPS
        o_b = nsa_attention(qb.reshape(B, S, NSA_GROUPS, R, NSA_DIM),
                            kvb.reshape(B, S, 6, NSA_GROUPS, NSA_DIM),
                            jax.nn.sigmoid(gb).reshape(B, S, NSA_GROUPS, R, 3),
                            nsa_cmp_pos[l], nsa_ck_w1[l], nsa_ck_w2[l], nsa_cv_w1[l], nsa_cv_w2[l])
        merged = (jax.nn.sigmoid(gate_a) * (o_a @ w_branch_a[l])
                  + jax.nn.sigmoid(gate_b) * (o_b @ w_branch_b[l]))
        h = h + merged @ w_out[l]
        h = h + memory_cross_attention(rms_norm(h, norm_cross[l]), rms_norm(mem, norm_mem[l]),
                                       w_cross_q[l], w_cross_kv[l], w_cross_o[l])
        h = h + peer_ffn(rms_norm(h, norm_ffn[l]), peer_wq[l], peer_subkeys[l], peer_u[l], peer_v[l])
    return rms_norm(h, norm_final)
```

```python
import functools

import jax
import jax.numpy as jnp
import numpy as np
from jax import lax
from jax.experimental import pallas as pl
from jax.experimental.pallas import tpu as pltpu

F32 = jnp.float32
BF16 = jnp.bfloat16

D_MODEL = 1024
DA_HEADS = 4
DA_DIM = 64
NSA_HEADS = 8
NSA_GROUPS = 2
NSA_REP = NSA_HEADS // NSA_GROUPS
NSA_DIM = 64
NSA_CMP_BLOCK = 32
NSA_SEL_BLOCK = 64
NSA_TOPN = 16
NSA_WINDOW = 512
NSA_CMP_HIDDEN = 256
CA_HEADS = 4
CA_DIM = 128
PEER_HEADS = 8
PEER_NKEYS = 128
PEER_DKEY = 256
PEER_TOPK = 16

DA_QK = DA_HEADS * 2 * DA_DIM
DA_V = DA_HEADS * 2 * DA_DIM
NSA_Q = NSA_HEADS * NSA_DIM
NSA_KV = 6 * NSA_GROUPS * NSA_DIM
NSA_G = NSA_HEADS * 3
IN_SIZES = (DA_QK, DA_QK, DA_V, NSA_Q, NSA_KV, NSA_G, D_MODEL, D_MODEL)

EPS = 1e-6
NEG = -1e30
V7X_VMEM_LIMIT = 48 * 1024 * 1024

NB_COLS = DA_QK * 2 + DA_V + NSA_Q + NSA_KV
NG_COLS = 2 * D_MODEL + 3 * NSA_Q


def _params(sem):
    return pltpu.CompilerParams(dimension_semantics=sem, vmem_limit_bytes=V7X_VMEM_LIMIT)


def _rms(x, g):
    return x * lax.rsqrt(jnp.mean(x * x, axis=-1, keepdims=True) + EPS) * g


def _dot_nt(a, b):
    return lax.dot_general(a, b, (((1,), (1,)), ((), ())), preferred_element_type=F32)


def _dot(a, b):
    return jnp.dot(a, b, preferred_element_type=F32)


def _gelu(x):
    return 0.5 * x * (1.0 + lax.erf(x * np.float32(1.0 / np.sqrt(2.0))))


def _sigmoid(x):
    return 1.0 / (1.0 + jnp.exp(-x))


def _in_proj_kernel(x_ref, g_ref, w_ref, ob_ref, og_ref, *, chunk):
    u = _rms(x_ref[...], g_ref[...]).astype(BF16)
    for c0 in range(0, NB_COLS, chunk):
        c1 = min(c0 + chunk, NB_COLS)
        ob_ref[:, c0:c1] = _dot(u, w_ref[:, c0:c1]).astype(BF16)
    for c0 in range(0, NG_COLS, chunk):
        c1 = min(c0 + chunk, NG_COLS)
        og_ref[:, c0:c1] = _dot(u, w_ref[:, NB_COLS + c0:NB_COLS + c1])


def _in_proj(x2, g, w, tm=512):
    T, D = x2.shape
    N = w.shape[1]
    return pl.pallas_call(
        functools.partial(_in_proj_kernel, chunk=512),
        grid=(T // tm,),
        in_specs=[pl.BlockSpec((tm, D), lambda i: (i, 0)),
                  pl.BlockSpec((1, D), lambda i: (0, 0)),
                  pl.BlockSpec((D, N), lambda i: (0, 0))],
        out_specs=[pl.BlockSpec((tm, NB_COLS), lambda i: (i, 0)),
                   pl.BlockSpec((tm, NG_COLS), lambda i: (i, 0))],
        out_shape=[jax.ShapeDtypeStruct((T, NB_COLS), BF16),
                   jax.ShapeDtypeStruct((T, NG_COLS), F32)],
        compiler_params=_params(("parallel",)),
        name="in_proj",
    )(x2, g, w)


def _softmax_step(s, v, m_ref, l_ref, acc_ref):
    m_old = m_ref[...]
    m_new = jnp.maximum(m_old, jnp.max(s, axis=-1, keepdims=True))
    p = jnp.exp(s - m_new)
    alpha = jnp.exp(m_old - m_new)
    l_ref[...] = alpha * l_ref[...] + jnp.sum(p, axis=-1, keepdims=True)
    acc_ref[...] = alpha * acc_ref[...] + _dot(p.astype(BF16), v)
    m_ref[...] = m_new


def _diff_attn_kernel(lam_ref, sub_ref, q_ref, k_ref, v_ref, o_ref, m_sc, l_sc, acc_sc,
                      *, tq, tk, lambda_init):
    h = pl.program_id(1)
    qi = pl.program_id(2)
    qstart = qi * tq
    slope = jnp.exp2(-2.0 * (h + 1).astype(F32))
    m_sc[...] = jnp.full(m_sc.shape, NEG, F32)
    l_sc[...] = jnp.zeros(l_sc.shape, F32)
    acc_sc[...] = jnp.zeros(acc_sc.shape, F32)
    qpos = qstart + lax.broadcasted_iota(jnp.int32, (tq, 1), 0)

    def chunk(j, masked):
        ks = pl.multiple_of(j * tk, tk)
        kpos = ks + lax.broadcasted_iota(jnp.int32, (1, tk), 1)
        bias = slope * (kpos - qstart).astype(F32)
        v = v_ref[0, 0, pl.ds(ks, tk), :]
        for c in range(2):
            s = _dot_nt(q_ref[0, 0, c], k_ref[0, 0, c, pl.ds(ks, tk), :]) + bias
            if masked:
                s = jnp.where(kpos <= qpos, s, NEG)
            _softmax_step(s, v, m_sc.at[c], l_sc.at[c], acc_sc.at[c])

    n_full = qstart // tk
    n_tot = (qstart + tq + tk - 1) // tk

    def full_body(j, carry):
        chunk(j, False)
        return carry

    def diag_body(j, carry):
        chunk(j, True)
        return carry

    lax.fori_loop(0, n_full, full_body, 0)
    lax.fori_loop(n_full, n_tot, diag_body, 0)

    lp = lam_ref[...]
    lam = (jnp.exp(jnp.sum(lp[0:1] * lp[1:2], axis=-1, keepdims=True))
           - jnp.exp(jnp.sum(lp[2:3] * lp[3:4], axis=-1, keepdims=True)) + lambda_init)
    o = acc_sc[0] / l_sc[0] - lam * (acc_sc[1] / l_sc[1])
    o_ref[0] = (_rms(o, sub_ref[...]) * (1.0 - lambda_init)).astype(o_ref.dtype)


def _diff_attn(q, k, v, lam_params, subln, lambda_init, tq=256, tk=512):
    B, H, _, S, d = q.shape
    dv = v.shape[-1]
    tq = min(tq, S)
    tk = min(tk, S)
    return pl.pallas_call(
        functools.partial(_diff_attn_kernel, tq=tq, tk=tk, lambda_init=lambda_init),
        grid=(B, H, S // tq),
        in_specs=[pl.BlockSpec((4, d), lambda b, h, i: (0, 0)),
                  pl.BlockSpec((1, dv), lambda b, h, i: (0, 0)),
                  pl.BlockSpec((1, 1, 2, tq, d), lambda b, h, i: (b, h, 0, i, 0)),
                  pl.BlockSpec((1, 1, 2, S, d), lambda b, h, i: (b, h, 0, 0, 0)),
                  pl.BlockSpec((1, 1, S, dv), lambda b, h, i: (b, h, 0, 0))],
        out_specs=pl.BlockSpec((1, tq, dv), lambda b, h, i: (b, i, h)),
        out_shape=jax.ShapeDtypeStruct((B, S, H * dv), BF16),
        scratch_shapes=[pltpu.VMEM((2, tq, 1), F32), pltpu.VMEM((2, tq, 1), F32),
                        pltpu.VMEM((2, tq, dv), F32)],
        compiler_params=_params(("parallel", "parallel", "arbitrary")),
        name="diff_attn",
    )(lam_params, subln, q, k, v)


def _compress_kernel(x_ref, pos_ref, w1_ref, w2_ref, o_ref):
    tb = (x_ref[0].astype(F32) + pos_ref[...]).astype(BF16)
    hid = _gelu(_dot(tb, w1_ref[0]))
    o_ref[0] = _dot(hid.astype(BF16), w2_ref[0]).astype(o_ref.dtype)


def _compress(xkv, pos, w1, w2):
    _, rows, kdim = xkv.shape
    hid = w1.shape[-1]
    dh = w2.shape[-1]
    return pl.pallas_call(
        _compress_kernel,
        grid=(2,),
        in_specs=[pl.BlockSpec((1, rows, kdim), lambda j: (j, 0, 0)),
                  pl.BlockSpec((1, kdim), lambda j: (0, 0)),
                  pl.BlockSpec((1, kdim, hid), lambda j: (j, 0, 0)),
                  pl.BlockSpec((1, hid, dh), lambda j: (j, 0, 0))],
        out_specs=pl.BlockSpec((1, rows, dh), lambda j: (j, 0, 0)),
        out_shape=jax.ShapeDtypeStruct((2, rows, dh), BF16),
        compiler_params=_params(("parallel",)),
        name="nsa_compress",
    )(xkv, pos, w1, w2)


def _nsa_slope(g, r):
    return jnp.exp2(-(4 * g + r + 1).astype(F32))


def _cmp_kernel(q_ref, kc_ref, vc_ref, o_ref, mneg_ref, *, tq, nc):
    g = pl.program_id(1)
    qi = pl.program_id(2)
    nsb = nc // 2
    q = q_ref[0, 0].reshape(NSA_REP * tq, NSA_DIM)
    s_all = _dot_nt(q, kc_ref[0, 0])
    col = lax.broadcasted_iota(jnp.int32, (1, nc), 1)
    cblk = jnp.where(col < nsb, 2 * col, 2 * (col - nsb) + 1)
    cpos = cblk * NSA_CMP_BLOCK + (NSA_CMP_BLOCK - 1)
    pos = qi * tq + lax.broadcasted_iota(jnp.int32, (tq, 1), 0)
    dist = (pos - cpos).astype(F32)
    valid = dist >= 0
    p_slc = jnp.zeros((tq, nsb), F32)
    for r in range(NSA_REP):
        s = s_all[r * tq:(r + 1) * tq] - _nsa_slope(g, jnp.int32(r)) * dist
        s = jnp.where(valid, s, NEG)
        e = jnp.exp(s - jnp.max(s, axis=-1, keepdims=True))
        p = jnp.where(valid, e / jnp.sum(e, axis=-1, keepdims=True), 0.0)
        o_ref[0, 0, r] = _dot(p.astype(BF16), vc_ref[0, 0])
        p_slc = p_slc + (p[:, :nsb] + p[:, nsb:])

    blk = lax.broadcasted_iota(jnp.int32, (1, nsb), 1)
    cur = pos // NSA_SEL_BLOCK
    forced = (blk == 0) | (blk == cur) | (blk == cur - 1)
    score = jnp.where(forced, 1e4, jnp.where(blk <= cur, p_slc, -1e9))
    sel = jnp.zeros((tq, nsb), jnp.bool_)
    for _ in range(min(NSA_TOPN, nsb)):
        mx = jnp.max(score, axis=-1, keepdims=True)
        idx = jnp.min(jnp.where(score == mx, blk, nsb), axis=-1, keepdims=True)
        hit = blk == idx
        sel = sel | hit
        score = jnp.where(hit, -jnp.inf, score)
    mneg_ref[0, 0] = jnp.where(sel, 0.0, NEG).astype(mneg_ref.dtype)


def _nsa_cmp(q, kc, vc, tq=256):
    B, G, R, S, dh = q.shape
    nc = kc.shape[2]
    tq = min(tq, S)
    return pl.pallas_call(
        functools.partial(_cmp_kernel, tq=tq, nc=nc),
        grid=(B, G, S // tq),
        in_specs=[pl.BlockSpec((1, 1, R, tq, dh), lambda b, g, i: (b, g, 0, i, 0)),
                  pl.BlockSpec((1, 1, nc, dh), lambda b, g, i: (b, g, 0, 0)),
                  pl.BlockSpec((1, 1, nc, dh), lambda b, g, i: (b, g, 0, 0))],
        out_specs=[pl.BlockSpec((1, 1, R, tq, dh), lambda b, g, i: (b, g, 0, i, 0)),
                   pl.BlockSpec((1, 1, tq, nc // 2), lambda b, g, i: (b, g, i, 0))],
        out_shape=[jax.ShapeDtypeStruct((B, G, R, S, dh), F32),
                   jax.ShapeDtypeStruct((B, G, S, nc // 2), BF16)],
        compiler_params=_params(("parallel", "parallel", "parallel")),
        name="nsa_cmp",
    )(q, kc, vc)


def _slc_kernel(q_ref, k_ref, v_ref, mneg_ref, o_ref, m_sc, l_sc, acc_sc, *, tq, tk, nsb):
    g = pl.program_id(1)
    qi = pl.program_id(2)
    qstart = qi * tq
    R = NSA_REP
    q = q_ref[0, 0].reshape(R * tq, NSA_DIM)
    mneg = mneg_ref[0, 0]
    mneg4 = jnp.concatenate([mneg] * R, axis=0)
    m_sc[...] = jnp.full(m_sc.shape, NEG, F32)
    l_sc[...] = jnp.zeros(l_sc.shape, F32)
    acc_sc[...] = jnp.zeros(acc_sc.shape, F32)
    qpos = qstart + lax.broadcasted_iota(jnp.int32, (tq, 1), 0)
    blk_row = lax.broadcasted_iota(jnp.int32, (nsb, 1), 0)

    def chunk(j, masked):
        ks = pl.multiple_of(j * tk, tk)
        kpos = ks + lax.broadcasted_iota(jnp.int32, (1, tk), 1)
        onehot = (blk_row == kpos // NSA_SEL_BLOCK).astype(BF16)
        s_all = _dot_nt(q, k_ref[0, 0, pl.ds(ks, tk), :]) + _dot(mneg4, onehot)
        krel = (kpos - qstart).astype(F32)
        parts = []
        for r in range(R):
            s = s_all[r * tq:(r + 1) * tq] + _nsa_slope(g, jnp.int32(r)) * krel
            if masked:
                s = jnp.where(kpos <= qpos, s, NEG)
            parts.append(s)
        s = jnp.concatenate(parts, axis=0)
        _softmax_step(s, v_ref[0, 0, pl.ds(ks, tk), :], m_sc, l_sc, acc_sc)

    n_full = qstart // tk
    n_tot = (qstart + tq + tk - 1) // tk

    def full_body(j, carry):
        chunk(j, False)
        return carry

    def diag_body(j, carry):
        chunk(j, True)
        return carry

    lax.fori_loop(0, n_full, full_body, 0)
    lax.fori_loop(n_full, n_tot, diag_body, 0)
    o_ref[0, 0] = (acc_sc[...] / l_sc[...]).reshape(R, tq, NSA_DIM)


def _nsa_slc(q, k, v, mneg, tq=128, tk=512):
    B, G, R, S, dh = q.shape
    nsb = mneg.shape[-1]
    tq = min(tq, S)
    tk = min(tk, S)
    return pl.pallas_call(
        functools.partial(_slc_kernel, tq=tq, tk=tk, nsb=nsb),
        grid=(B, G, S // tq),
        in_specs=[pl.BlockSpec((1, 1, R, tq, dh), lambda b, g, i: (b, g, 0, i, 0)),
                  pl.BlockSpec((1, 1, S, dh), lambda b, g, i: (b, g, 0, 0)),
                  pl.BlockSpec((1, 1, S, dh), lambda b, g, i: (b, g, 0, 0)),
                  pl.BlockSpec((1, 1, tq, nsb), lambda b, g, i: (b, g, i, 0))],
        out_specs=pl.BlockSpec((1, 1, R, tq, dh), lambda b, g, i: (b, g, 0, i, 0)),
        out_shape=jax.ShapeDtypeStruct((B, G, R, S, dh), F32),
        scratch_shapes=[pltpu.VMEM((R * tq, 1), F32), pltpu.VMEM((R * tq, 1), F32),
                        pltpu.VMEM((R * tq, dh), F32)],
        compiler_params=_params(("parallel", "parallel", "arbitrary")),
        name="nsa_slc",
    )(q, k, v, mneg)


def _win_kernel(q_ref, k_ref, v_ref, o_ref, *, tq):
    g = pl.program_id(1)
    qi = pl.program_id(2)
    qstart = qi * tq
    R = NSA_REP
    span = tq + NSA_WINDOW
    q = q_ref[0, 0].reshape(R * tq, NSA_DIM)
    ks = pl.multiple_of(qstart, tq)
    s_all = _dot_nt(q, k_ref[0, 0, pl.ds(ks, span), :])
    kpos = qstart - NSA_WINDOW + lax.broadcasted_iota(jnp.int32, (1, span), 1)
    qpos = qstart + lax.broadcasted_iota(jnp.int32, (tq, 1), 0)
    dist = qpos - kpos
    valid = (dist >= 0) & (dist < NSA_WINDOW) & (kpos >= 0)
    distf = dist.astype(F32)
    parts = []
    for r in range(R):
        s = s_all[r * tq:(r + 1) * tq] - _nsa_slope(g, jnp.int32(r)) * distf
        s = jnp.where(valid, s, NEG)
        e = jnp.exp(s - jnp.max(s, axis=-1, keepdims=True))
        parts.append(e / jnp.sum(e, axis=-1, keepdims=True))
    p = jnp.concatenate(parts, axis=0).astype(BF16)
    o_ref[0, 0] = _dot(p, v_ref[0, 0, pl.ds(ks, span), :]).reshape(R, tq, NSA_DIM)


def _nsa_win(q, kpad, vpad, tq=256):
    B, G, R, S, dh = q.shape
    SP = kpad.shape[2]
    tq = min(tq, S)
    return pl.pallas_call(
        functools.partial(_win_kernel, tq=tq),
        grid=(B, G, S // tq),
        in_specs=[pl.BlockSpec((1, 1, R, tq, dh), lambda b, g, i: (b, g, 0, i, 0)),
                  pl.BlockSpec((1, 1, SP, dh), lambda b, g, i: (b, g, 0, 0)),
                  pl.BlockSpec((1, 1, SP, dh), lambda b, g, i: (b, g, 0, 0))],
        out_specs=pl.BlockSpec((1, 1, R, tq, dh), lambda b, g, i: (b, g, 0, i, 0)),
        out_shape=jax.ShapeDtypeStruct((B, G, R, S, dh), F32),
        compiler_params=_params(("parallel", "parallel", "parallel")),
        name="nsa_win",
    )(q, kpad, vpad)


def _mix_out_kernel(x_ref, oa_ref, oc_ref, os_ref, ow_ref, pg_ref, wa_ref, wb_ref, wo_ref, h_ref):
    D = D_MODEL
    nq = NSA_Q
    gl = 2 * D
    o_b = (_sigmoid(pg_ref[:, gl:gl + nq]) * oc_ref[...]
           + _sigmoid(pg_ref[:, gl + nq:gl + 2 * nq]) * os_ref[...]
           + _sigmoid(pg_ref[:, gl + 2 * nq:gl + 3 * nq]) * ow_ref[...])
    merged = (_sigmoid(pg_ref[:, 0:D]) * _dot(oa_ref[...], wa_ref[...])
              + _sigmoid(pg_ref[:, D:2 * D]) * _dot(o_b.astype(BF16), wb_ref[...]))
    h_ref[...] = x_ref[...] + _dot(merged.astype(BF16), wo_ref[...])


def _mix_out(x2, oa, oc, osl, ow, pg, wa, wb, wo, tm=256):
    T, D = x2.shape
    row = lambda i: (i, 0)
    const = lambda i: (0, 0)
    return pl.pallas_call(
        _mix_out_kernel,
        grid=(T // tm,),
        in_specs=[pl.BlockSpec((tm, D), row), pl.BlockSpec((tm, DA_V), row),
                  pl.BlockSpec((tm, NSA_Q), row), pl.BlockSpec((tm, NSA_Q), row),
                  pl.BlockSpec((tm, NSA_Q), row), pl.BlockSpec((tm, NG_COLS), row),
                  pl.BlockSpec((DA_V, D), const), pl.BlockSpec((NSA_Q, D), const),
                  pl.BlockSpec((D, D), const)],
        out_specs=pl.BlockSpec((tm, D), row),
        out_shape=jax.ShapeDtypeStruct((T, D), F32),
        compiler_params=_params(("parallel",)),
        name="mix_out",
    )(x2, oa, oc, osl, ow, pg, wa, wb, wo)


def _mem_kv_kernel(m_ref, g_ref, w_ref, o_ref):
    o_ref[...] = _dot(_rms(m_ref[...], g_ref[...]).astype(BF16), w_ref[...]).astype(o_ref.dtype)


def _mem_kv(mem2, g, w):
    M, D = mem2.shape
    N = w.shape[1]
    return pl.pallas_call(
        _mem_kv_kernel,
        grid=(1,),
        in_specs=[pl.BlockSpec((M, D), lambda i: (0, 0)), pl.BlockSpec((1, D), lambda i: (0, 0)),
                  pl.BlockSpec((D, N), lambda i: (0, 0))],
        out_specs=pl.BlockSpec((M, N), lambda i: (0, 0)),
        out_shape=jax.ShapeDtypeStruct((M, N), BF16),
        compiler_params=_params(("arbitrary",)),
        name="mem_kv",
    )(mem2, g, w)


def _cross_kernel(h_ref, g_ref, wq_ref, kv_ref, wo_ref, o_ref):
    h = h_ref[0]
    u = _rms(h, g_ref[...]).astype(BF16)
    q = (_dot(u, wq_ref[...]) * np.float32(CA_DIM ** -0.5)).astype(BF16)
    hd = CA_HEADS * CA_DIM
    outs = []
    for a in range(CA_HEADS):
        kh = kv_ref[0, :, a * CA_DIM:(a + 1) * CA_DIM]
        vh = kv_ref[0, :, hd + a * CA_DIM:hd + (a + 1) * CA_DIM]
        s = _dot_nt(q[:, a * CA_DIM:(a + 1) * CA_DIM], kh)
        e = jnp.exp(s - jnp.max(s, axis=-1, keepdims=True))
        p = e / jnp.sum(e, axis=-1, keepdims=True)
        outs.append(_dot(p.astype(BF16), vh))
    o = jnp.concatenate(outs, axis=-1).astype(BF16)
    o_ref[0] = h + _dot(o, wo_ref[...])


def _cross_attn(h, g, wq, kv, wo, tm=256):
    B, S, D = h.shape
    M = kv.shape[1]
    hd = CA_HEADS * CA_DIM
    return pl.pallas_call(
        _cross_kernel,
        grid=(B, S // tm),
        in_specs=[pl.BlockSpec((1, tm, D), lambda b, i: (b, i, 0)),
                  pl.BlockSpec((1, D), lambda b, i: (0, 0)),
                  pl.BlockSpec((D, hd), lambda b, i: (0, 0)),
                  pl.BlockSpec((1, M, 2 * hd), lambda b, i: (b, 0, 0)),
                  pl.BlockSpec((hd, D), lambda b, i: (0, 0))],
        out_specs=pl.BlockSpec((1, tm, D), lambda b, i: (b, i, 0)),
        out_shape=jax.ShapeDtypeStruct((B, S, D), F32),
        compiler_params=_params(("parallel", "parallel")),
        name="cross_attn",
    )(h, g, wq, kv, wo)


def _peer_query_kernel(h_ref, g_ref, wqt_ref, sk_ref, u_ref, s_ref):
    u = _rms(h_ref[...], g_ref[...]).astype(BF16)
    u_ref[...] = u
    half = PEER_DKEY // 2
    for hc in range(2 * PEER_HEADS):
        qt = _dot_nt(wqt_ref[hc * half:(hc + 1) * half, :], u)
        s_ref[hc] = _dot(sk_ref[hc], qt.astype(BF16))


def _peer_query(h2, g, wqt, sk, tm=256):
    T, D = h2.shape
    HC = sk.shape[0]
    return pl.pallas_call(
        _peer_query_kernel,
        grid=(T // tm,),
        in_specs=[pl.BlockSpec((tm, D), lambda i: (i, 0)), pl.BlockSpec((1, D), lambda i: (0, 0)),
                  pl.BlockSpec(wqt.shape, lambda i: (0, 0)),
                  pl.BlockSpec(sk.shape, lambda i: (0, 0, 0))],
        out_specs=[pl.BlockSpec((tm, D), lambda i: (i, 0)),
                   pl.BlockSpec((HC, PEER_NKEYS, tm), lambda i: (0, 0, i))],
        out_shape=[jax.ShapeDtypeStruct((T, D), BF16),
                   jax.ShapeDtypeStruct((HC, PEER_NKEYS, T), F32)],
        compiler_params=_params(("parallel",)),
        name="peer_query",
    )(h2, g, wqt, sk)


def _top_rows(x, k):
    n = x.shape[0]
    row = lax.broadcasted_iota(jnp.int32, x.shape, 0)
    sel = jnp.zeros(x.shape, jnp.bool_)
    vals = []
    for _ in range(k):
        mx = jnp.max(x, axis=0, keepdims=True)
        idx = jnp.min(jnp.where(x == mx, row, n), axis=0, keepdims=True)
        hit = row == idx
        sel = sel | hit
        x = jnp.where(hit, -jnp.inf, x)
        vals.append(mx)
    return vals, sel


def _peer_route_kernel(s_ref, s0_ref, a_ref, s1_ref, b_ref, tau_ref):
    K = PEER_TOPK
    s0 = s_ref[0]
    s1 = s_ref[1]
    v0, sel0 = _top_rows(s0, K)
    v1, sel1 = _top_rows(s1, K)
    top1 = jnp.concatenate(v1, axis=0)
    cand = jnp.concatenate([v0[i] + top1 for i in range(K)], axis=0)
    cvals, _ = _top_rows(cand, K)
    best = cvals[0]
    tau = cvals[K - 1]
    z = cvals[0] - best
    z = jnp.exp(z)
    for i in range(1, K):
        z = z + jnp.exp(cvals[i] - best)
    tau_ref[0] = tau
    s0_ref[0] = jnp.where(sel0, s0, -jnp.inf)
    a_ref[0] = jnp.where(sel0, jnp.exp(s0 - v0[0]) / z, 0.0)
    s1_ref[0] = jnp.where(sel1, s1, -jnp.inf)
    b_ref[0] = jnp.where(sel1, jnp.exp(s1 - v1[0]), 0.0)


def _peer_route(st, tl=256):
    HC, NK, T = st.shape
    H = HC // 2
    spec = pl.BlockSpec((1, NK, tl), lambda h, i: (h, 0, i))
    shp = jax.ShapeDtypeStruct((H, NK, T), F32)
    return pl.pallas_call(
        _peer_route_kernel,
        grid=(H, T // tl),
        in_specs=[pl.BlockSpec((2, NK, tl), lambda h, i: (h, 0, i))],
        out_specs=[spec, spec, spec, spec, pl.BlockSpec((1, 1, tl), lambda h, i: (h, 0, i))],
        out_shape=[shp, shp, shp, shp, jax.ShapeDtypeStruct((H, 1, T), F32)],
        compiler_params=_params(("parallel", "parallel")),
        name="peer_route",
    )(st)


def _peer_expert_kernel(u_ref, ut_ref, vt_ref, s0_ref, a_ref, s1_ref, b_ref, tau_ref, o_ref, *, rows):
    e = pl.program_id(1)
    nk = PEER_NKEYS

    @pl.when(e == 0)
    def _():
        o_ref[...] = jnp.zeros(o_ref.shape, F32)

    act = _gelu(_dot_nt(ut_ref[...], u_ref[...]))
    parts = []
    for ii in range(rows):
        i = e * rows + ii
        gate = jnp.zeros((nk, act.shape[1]), F32)
        for h in range(PEER_HEADS):
            s0 = s0_ref[h, pl.ds(i, 1), :]
            a = a_ref[h, pl.ds(i, 1), :]
            gate = gate + jnp.where(s0 + s1_ref[h] >= tau_ref[h], b_ref[h] * a, 0.0)
        parts.append((gate * act[ii * nk:(ii + 1) * nk]).astype(BF16))
    w = jnp.concatenate(parts, axis=0)
    o_ref[...] += _dot(vt_ref[...], w)


def _peer_expert(u3, ut, vt, s0, a, s1, b, tau, tt=512, rows=4):
    T, D = u3.shape
    NE = ut.shape[0]
    H, NK, _ = s0.shape
    te = rows * NK
    tt = min(tt, T)
    rspec = pl.BlockSpec((H, NK, tt), lambda t, e: (0, 0, t))
    return pl.pallas_call(
        functools.partial(_peer_expert_kernel, rows=rows),
        grid=(T // tt, NE // te),
        in_specs=[pl.BlockSpec((tt, D), lambda t, e: (t, 0)),
                  pl.BlockSpec((te, D), lambda t, e: (e, 0)),
                  pl.BlockSpec((D, te), lambda t, e: (0, e)),
                  rspec, rspec, rspec, rspec,
                  pl.BlockSpec((H, 1, tt), lambda t, e: (0, 0, t))],
        out_specs=pl.BlockSpec((D, tt), lambda t, e: (0, t)),
        out_shape=jax.ShapeDtypeStruct((D, T), F32),
        compiler_params=_params(("parallel", "arbitrary")),
        name="peer_expert",
    )(u3, ut, vt, s0, a, s1, b, tau)


def _final_kernel(h_ref, pt_ref, g_ref, o_ref):
    o_ref[...] = _rms(h_ref[...] + pt_ref[...].T, g_ref[...])


def _final_norm(h2, pt, g, tm=256):
    T, D = h2.shape
    return pl.pallas_call(
        _final_kernel,
        grid=(T // tm,),
        in_specs=[pl.BlockSpec((tm, D), lambda i: (i, 0)), pl.BlockSpec((D, tm), lambda i: (0, i)),
                  pl.BlockSpec((1, D), lambda i: (0, 0))],
        out_specs=pl.BlockSpec((tm, D), lambda i: (i, 0)),
        out_shape=jax.ShapeDtypeStruct((T, D), F32),
        compiler_params=_params(("parallel",)),
        name="final_norm",
    )(h2, pt, g)


def _regroup_w_in(w):
    qa, ka, va, qb, kvb, gb, ga, gbm = jnp.split(w, np.cumsum(IN_SIZES)[:-1].tolist(), axis=-1)
    gbe = gb.reshape(-1, NSA_HEADS, 3).transpose(0, 2, 1)
    gbe = jnp.repeat(gbe, NSA_DIM, axis=-1)
    gbe = gbe.reshape(w.shape[0], 3 * NSA_Q)
    return jnp.concatenate([qa * DA_DIM ** -0.5, ka, va, qb * NSA_DIM ** -0.5, kvb, ga, gbm, gbe],
                           axis=-1).astype(BF16)


def _layer(h, mem, l, p):
    B, S, D = h.shape
    T = B * S
    G, R, dh = NSA_GROUPS, NSA_REP, NSA_DIM
    lambda_init = 0.8 - 0.6 * float(np.exp(-0.3 * l))
    x2 = h.reshape(T, D)

    pb, pg = _in_proj(x2, p["norm_mix"][l][None], _regroup_w_in(p["w_in"][l]))
    pb = pb.reshape(B, S, NB_COLS)
    c = 0
    qa = pb[..., c:c + DA_QK].reshape(B, S, DA_HEADS, 2, DA_DIM).transpose(0, 2, 3, 1, 4); c += DA_QK
    ka = pb[..., c:c + DA_QK].reshape(B, S, DA_HEADS, 2, DA_DIM).transpose(0, 2, 3, 1, 4); c += DA_QK
    va = pb[..., c:c + DA_V].reshape(B, S, DA_HEADS, 2 * DA_DIM).transpose(0, 2, 1, 3); c += DA_V
    qb = pb[..., c:c + NSA_Q].reshape(B, S, G, R, dh).transpose(0, 2, 3, 1, 4); c += NSA_Q
    kvb = pb[..., c:c + NSA_KV].reshape(B, S, 6, G, dh).transpose(2, 0, 3, 1, 4)

    o_a = _diff_attn(qa, ka, va, p["diff_lambda"][l], p["diff_subln"][l][None], lambda_init)

    nc = S // NSA_CMP_BLOCK
    kdim = NSA_CMP_BLOCK * dh
    xkv = kvb[0:2].reshape(2, B * G * nc, kdim)
    w1 = jnp.stack([p["nsa_ck_w1"][l], p["nsa_cv_w1"][l]]).astype(BF16)
    w2 = jnp.stack([p["nsa_ck_w2"][l], p["nsa_cv_w2"][l]]).astype(BF16)
    kvc = _compress(xkv, p["nsa_cmp_pos"][l].reshape(1, kdim), w1, w2)
    kvc = kvc.reshape(2, B, G, nc // 2, 2, dh).transpose(0, 1, 2, 4, 3, 5).reshape(2, B, G, nc, dh)
    o_cmp, mneg = _nsa_cmp(qb, kvc[0], kvc[1])
    o_slc = _nsa_slc(qb, kvb[2], kvb[3], mneg)
    pad = ((0, 0), (0, 0), (NSA_WINDOW, 0), (0, 0))
    o_win = _nsa_win(qb, jnp.pad(kvb[4], pad), jnp.pad(kvb[5], pad))

    def tok_major(o):
        return o.transpose(0, 3, 1, 2, 4).reshape(T, NSA_Q)

    h1 = _mix_out(x2, o_a.reshape(T, DA_V), tok_major(o_cmp), tok_major(o_slc), tok_major(o_win), pg,
                  p["w_branch_a"][l].astype(BF16), p["w_branch_b"][l].astype(BF16),
                  p["w_out"][l].astype(BF16))

    M = mem.shape[1]
    kvm = _mem_kv(mem.reshape(B * M, D), p["norm_mem"][l][None], p["w_cross_kv"][l].astype(BF16))
    h2 = _cross_attn(h1.reshape(B, S, D), p["norm_cross"][l][None], p["w_cross_q"][l].astype(BF16),
                     kvm.reshape(B, M, -1), p["w_cross_o"][l].astype(BF16)).reshape(T, D)

    sk = p["peer_subkeys"][l].reshape(2 * PEER_HEADS, PEER_NKEYS, PEER_DKEY // 2).astype(BF16)
    u3, st = _peer_query(h2, p["norm_ffn"][l][None], p["peer_wq"][l].T.astype(BF16), sk)
    s0, a, s1, b, tau = _peer_route(st)
    pt = _peer_expert(u3, p["peer_u"][l].astype(BF16), p["peer_v"][l].T.astype(BF16), s0, a, s1, b, tau)
    return h2, pt


def kernel(x, mem, norm_mix, w_in, diff_lambda, diff_subln, nsa_cmp_pos, nsa_ck_w1, nsa_ck_w2,
           nsa_cv_w1, nsa_cv_w2, w_branch_a, w_branch_b, w_out, norm_cross, norm_mem,
           w_cross_q, w_cross_kv, w_cross_o, norm_ffn, peer_wq, peer_subkeys, peer_u, peer_v,
           norm_final):
    p = dict(norm_mix=norm_mix, w_in=w_in, diff_lambda=diff_lambda, diff_subln=diff_subln,
             nsa_cmp_pos=nsa_cmp_pos, nsa_ck_w1=nsa_ck_w1, nsa_ck_w2=nsa_ck_w2, nsa_cv_w1=nsa_cv_w1,
             nsa_cv_w2=nsa_cv_w2, w_branch_a=w_branch_a, w_branch_b=w_branch_b, w_out=w_out,
             norm_cross=norm_cross, norm_mem=norm_mem, w_cross_q=w_cross_q, w_cross_kv=w_cross_kv,
             w_cross_o=w_cross_o, norm_ffn=norm_ffn, peer_wq=peer_wq, peer_subkeys=peer_subkeys,
             peer_u=peer_u, peer_v=peer_v)
    B, S, D = x.shape
    depth = w_in.shape[0]
    assert depth == 1, "the last layer's PEER output is folded into the final norm"
    h2, pt = _layer(x, mem, 0, p)
    return _final_norm(h2, pt, norm_final[None]).reshape(B, S, D)
```

```python
import functools

import jax
import jax.numpy as jnp
import numpy as np
from jax import lax
from jax.experimental import pallas as pl
from jax.experimental.pallas import tpu as pltpu

F32 = jnp.float32
BF16 = jnp.bfloat16

D_MODEL = 1024
DA_HEADS = 4
DA_DIM = 64
NSA_HEADS = 8
NSA_GROUPS = 2
NSA_REP = NSA_HEADS // NSA_GROUPS
NSA_DIM = 64
NSA_CMP_BLOCK = 32
NSA_SEL_BLOCK = 64
NSA_TOPN = 16
NSA_WINDOW = 512
NSA_CMP_HIDDEN = 256
CA_HEADS = 4
CA_DIM = 128
PEER_HEADS = 8
PEER_NKEYS = 128
PEER_DKEY = 256
PEER_TOPK = 16

DA_QK = DA_HEADS * 2 * DA_DIM
DA_V = DA_HEADS * 2 * DA_DIM
NSA_Q = NSA_HEADS * NSA_DIM
NSA_KV = 6 * NSA_GROUPS * NSA_DIM
NSA_G = NSA_HEADS * 3
IN_SIZES = (DA_QK, DA_QK, DA_V, NSA_Q, NSA_KV, NSA_G, D_MODEL, D_MODEL)

EPS = 1e-6
NEG = -1e30
V7X_VMEM_LIMIT = 48 * 1024 * 1024

NB_COLS = DA_QK * 2 + DA_V + NSA_Q + NSA_KV
NG_COLS = 2 * D_MODEL + 3 * NSA_Q


def _params(sem, flags=None):
    return pltpu.CompilerParams(dimension_semantics=sem, vmem_limit_bytes=V7X_VMEM_LIMIT, flags=flags)


def _rms(x, g):
    return x * lax.rsqrt(jnp.mean(x * x, axis=-1, keepdims=True) + EPS) * g


def _dot_nt(a, b):
    return lax.dot_general(a, b, (((1,), (1,)), ((), ())), preferred_element_type=F32)


def _dot(a, b):
    return jnp.dot(a, b, preferred_element_type=F32)


def _gelu(x):
    return 0.5 * x * (1.0 + lax.erf(x * np.float32(1.0 / np.sqrt(2.0))))


def _sigmoid(x):
    return 1.0 / (1.0 + jnp.exp(-x))


def _in_proj_kernel(x_ref, g_ref, w_ref, ob_ref, og_ref, *, chunk):
    u = _rms(x_ref[...], g_ref[...]).astype(BF16)
    for c0 in range(0, NB_COLS, chunk):
        c1 = min(c0 + chunk, NB_COLS)
        ob_ref[:, c0:c1] = _dot(u, w_ref[:, c0:c1]).astype(BF16)
    for c0 in range(0, NG_COLS, chunk):
        c1 = min(c0 + chunk, NG_COLS)
        og_ref[:, c0:c1] = _dot(u, w_ref[:, NB_COLS + c0:NB_COLS + c1])


def _in_proj(x2, g, w, tm=512):
    T, D = x2.shape
    N = w.shape[1]
    return pl.pallas_call(
        functools.partial(_in_proj_kernel, chunk=512),
        grid=(T // tm,),
        in_specs=[pl.BlockSpec((tm, D), lambda i: (i, 0)),
                  pl.BlockSpec((1, D), lambda i: (0, 0)),
                  pl.BlockSpec((D, N), lambda i: (0, 0))],
        out_specs=[pl.BlockSpec((tm, NB_COLS), lambda i: (i, 0)),
                   pl.BlockSpec((tm, NG_COLS), lambda i: (i, 0))],
        out_shape=[jax.ShapeDtypeStruct((T, NB_COLS), BF16),
                   jax.ShapeDtypeStruct((T, NG_COLS), F32)],
        compiler_params=_params(("parallel",)),
        name="in_proj",
    )(x2, g, w)


def _flash_step(st, vt, m_ref, l_ref, acc_ref):
    m_old = m_ref[...]
    m_new = jnp.maximum(m_old, jnp.max(st, axis=0, keepdims=True))
    p = jnp.exp(st - m_new)
    alpha = jnp.exp(m_old - m_new)
    l_ref[...] = alpha * l_ref[...] + jnp.sum(p, axis=0, keepdims=True)
    acc_ref[...] = alpha * acc_ref[...] + _dot(vt, p.astype(BF16))
    m_ref[...] = m_new


def _causal_sweep(qstart, tq, tk, chunk):
    n_full = qstart // tk
    n_tot = (qstart + tq + tk - 1) // tk

    def full_body(j, carry):
        chunk(j, False)
        return carry

    def diag_body(j, carry):
        chunk(j, True)
        return carry

    lax.fori_loop(0, n_full, full_body, 0)
    lax.fori_loop(n_full, n_tot, diag_body, 0)


def _kpos_cols(S):
    kpos = jnp.arange(S, dtype=jnp.int32)
    return jnp.stack([(kpos // 256) * 256, kpos % 256], axis=-1).astype(BF16)


def _diff_attn_kernel(lam_ref, sub_ref, q_ref, k_ref, v_ref, o_ref, m_sc, l_sc, acc_sc,
                      *, tq, tk, lambda_init):
    qi = pl.program_id(2)
    qstart = qi * tq
    m_sc[...] = jnp.full(m_sc.shape, NEG, F32)
    l_sc[...] = jnp.zeros(l_sc.shape, F32)
    acc_sc[...] = jnp.zeros(acc_sc.shape, F32)
    qpos = qstart + lax.broadcasted_iota(jnp.int32, (1, tq), 1)

    def chunk(j, masked):
        ks = pl.multiple_of(j * tk, tk)
        vt = v_ref[0, 0, :, pl.ds(ks, tk)]
        if masked:
            keep = ks + lax.broadcasted_iota(jnp.int32, (tk, 1), 0) <= qpos
        for c in range(2):
            st = _dot(k_ref[0, 0, c, pl.ds(ks, tk), :], q_ref[0, 0, c])
            if masked:
                st = jnp.where(keep, st, NEG)
            _flash_step(st, vt, m_sc.at[c], l_sc.at[c], acc_sc.at[c])

    _causal_sweep(qstart, tq, tk, chunk)

    lp = lam_ref[...]
    lam = (jnp.exp(jnp.sum(lp[0:1] * lp[1:2], axis=-1, keepdims=True))
           - jnp.exp(jnp.sum(lp[2:3] * lp[3:4], axis=-1, keepdims=True)) + lambda_init)
    o = acc_sc[0] / l_sc[0] - lam * (acc_sc[1] / l_sc[1])
    y = o * lax.rsqrt(jnp.mean(o * o, axis=0, keepdims=True) + EPS) * sub_ref[...]
    o_ref[0] = (y * (1.0 - lambda_init)).T.astype(o_ref.dtype)


def _diff_attn(qt, k, vt, lam_params, subln, lambda_init, tq=512, tk=512):
    B, H, _, da, S = qt.shape
    dv = vt.shape[2]
    tq = min(tq, S)
    tk = min(tk, S)
    return pl.pallas_call(
        functools.partial(_diff_attn_kernel, tq=tq, tk=tk, lambda_init=lambda_init),
        grid=(B, H, S // tq),
        in_specs=[pl.BlockSpec(lam_params.shape, lambda b, h, i: (0, 0)),
                  pl.BlockSpec((dv, 1), lambda b, h, i: (0, 0)),
                  pl.BlockSpec((1, 1, 2, da, tq), lambda b, h, i: (b, h, 0, 0, i)),
                  pl.BlockSpec((1, 1, 2, S, da), lambda b, h, i: (b, h, 0, 0, 0)),
                  pl.BlockSpec((1, 1, dv, S), lambda b, h, i: (b, h, 0, 0))],
        out_specs=pl.BlockSpec((1, tq, dv), lambda b, h, i: (b, i, h)),
        out_shape=jax.ShapeDtypeStruct((B, S, H * dv), BF16),
        scratch_shapes=[pltpu.VMEM((2, 1, tq), F32), pltpu.VMEM((2, 1, tq), F32),
                        pltpu.VMEM((2, dv, tq), F32)],
        compiler_params=_params(("parallel", "parallel", "arbitrary")),
        name="diff_attn",
    )(lam_params, subln, qt, k, vt)


def _compress_kernel(x_ref, pos_ref, w1_ref, w2_ref, o_ref):
    tb = (x_ref[0].astype(F32) + pos_ref[...]).astype(BF16)
    hid = _gelu(_dot(tb, w1_ref[0]))
    o_ref[0] = _dot(hid.astype(BF16), w2_ref[0]).astype(o_ref.dtype)


def _compress(xkv, pos, w1, w2):
    _, rows, kdim = xkv.shape
    hid = w1.shape[-1]
    dh = w2.shape[-1]
    return pl.pallas_call(
        _compress_kernel,
        grid=(2,),
        in_specs=[pl.BlockSpec((1, rows, kdim), lambda j: (j, 0, 0)),
                  pl.BlockSpec((1, kdim), lambda j: (0, 0)),
                  pl.BlockSpec((1, kdim, hid), lambda j: (j, 0, 0)),
                  pl.BlockSpec((1, hid, dh), lambda j: (j, 0, 0))],
        out_specs=pl.BlockSpec((1, rows, dh), lambda j: (j, 0, 0)),
        out_shape=jax.ShapeDtypeStruct((2, rows, dh), BF16),
        compiler_params=_params(("parallel",)),
        name="nsa_compress",
    )(xkv, pos, w1, w2)


def _nsa_slope(g, r):
    return jnp.exp2(-(4 * g + r + 1).astype(F32))


def _cmp_kernel(q_ref, kc_ref, vc_ref, o_ref, mneg_ref, *, tq, nc):
    g = pl.program_id(1)
    qi = pl.program_id(2)
    nsb = nc // 2
    q = q_ref[0, 0].reshape(NSA_REP * tq, NSA_DIM)
    s_all = _dot_nt(q, kc_ref[0, 0])
    col = lax.broadcasted_iota(jnp.int32, (1, nc), 1)
    cblk = jnp.where(col < nsb, 2 * col, 2 * (col - nsb) + 1)
    cpos = cblk * NSA_CMP_BLOCK + (NSA_CMP_BLOCK - 1)
    pos = qi * tq + lax.broadcasted_iota(jnp.int32, (tq, 1), 0)
    dist = (pos - cpos).astype(F32)
    valid = dist >= 0
    p_slc = jnp.zeros((tq, nsb), F32)
    for r in range(NSA_REP):
        s = s_all[r * tq:(r + 1) * tq] - _nsa_slope(g, jnp.int32(r)) * dist
        s = jnp.where(valid, s, NEG)
        e = jnp.exp(s - jnp.max(s, axis=-1, keepdims=True))
        p = jnp.where(valid, e / jnp.sum(e, axis=-1, keepdims=True), 0.0)
        o_ref[0, 0, r] = _dot(p.astype(BF16), vc_ref[0, 0])
        p_slc = p_slc + (p[:, :nsb] + p[:, nsb:])

    blk = lax.broadcasted_iota(jnp.int32, (1, nsb), 1)
    cur = pos // NSA_SEL_BLOCK
    forced = (blk == 0) | (blk == cur) | (blk == cur - 1)
    score = jnp.where(forced, 1e4, jnp.where(blk <= cur, p_slc, -1e9))
    sel = jnp.zeros((tq, nsb), jnp.bool_)
    for _ in range(min(NSA_TOPN, nsb)):
        mx = jnp.max(score, axis=-1, keepdims=True)
        idx = jnp.min(jnp.where(score == mx, blk, nsb), axis=-1, keepdims=True)
        hit = blk == idx
        sel = sel | hit
        score = jnp.where(hit, -jnp.inf, score)
    mneg_ref[0, 0] = jnp.where(sel, 0.0, NEG).T.astype(mneg_ref.dtype)


def _nsa_cmp(q, kc, vc, tq=256):
    B, G, R, S, dh = q.shape
    nc = kc.shape[2]
    tq = min(tq, S)
    return pl.pallas_call(
        functools.partial(_cmp_kernel, tq=tq, nc=nc),
        grid=(B, G, S // tq),
        in_specs=[pl.BlockSpec((1, 1, R, tq, dh), lambda b, g, i: (b, g, 0, i, 0)),
                  pl.BlockSpec((1, 1, nc, dh), lambda b, g, i: (b, g, 0, 0)),
                  pl.BlockSpec((1, 1, nc, dh), lambda b, g, i: (b, g, 0, 0))],
        out_specs=[pl.BlockSpec((1, 1, R, tq, dh), lambda b, g, i: (b, g, 0, i, 0)),
                   pl.BlockSpec((1, 1, nc // 2, tq), lambda b, g, i: (b, g, 0, i))],
        out_shape=[jax.ShapeDtypeStruct((B, G, R, S, dh), F32),
                   jax.ShapeDtypeStruct((B, G, nc // 2, S), BF16)],
        compiler_params=_params(("parallel", "parallel", "parallel")),
        name="nsa_cmp",
    )(q, kc, vc)


def _slc_aug_width(nsb):
    return -(-(NSA_DIM + nsb + 16) // 128) * 128


def _slc_kernel(q_ref, k_ref, v_ref, mneg_ref, o_ref, aug_sc, m_sc, l_sc, acc_sc, *, tq, tk, nsb):
    g = pl.program_id(1)
    qi = pl.program_id(2)
    qstart = qi * tq
    R = NSA_REP
    n = R * tq
    SLC_BIAS_ROW = NSA_DIM + nsb
    SLC_KA = aug_sc.shape[0]
    aug_sc[0:NSA_DIM, :] = jnp.concatenate([q_ref[0, 0, r] for r in range(R)], axis=1)
    aug_sc[NSA_DIM:SLC_BIAS_ROW, :] = jnp.concatenate([mneg_ref[0, 0]] * R, axis=1)
    col = lax.broadcasted_iota(jnp.int32, (16, n), 1)
    slope = jnp.exp2(-(4 * g + col // tq + 1).astype(F32))
    is_bias = lax.broadcasted_iota(jnp.int32, (16, n), 0) < 2
    aug_sc[SLC_BIAS_ROW:SLC_BIAS_ROW + 16, :] = jnp.where(is_bias, slope, 0.0).astype(BF16)
    aug_sc[SLC_BIAS_ROW + 16:, :] = jnp.zeros((SLC_KA - SLC_BIAS_ROW - 16, n), BF16)
    m_sc[...] = jnp.full(m_sc.shape, NEG, F32)
    l_sc[...] = jnp.zeros(l_sc.shape, F32)
    acc_sc[...] = jnp.zeros(acc_sc.shape, F32)
    qpos = qstart + lax.broadcasted_iota(jnp.int32, (1, n), 1) % tq

    def chunk(j, masked):
        ks = pl.multiple_of(j * tk, tk)
        st = _dot(k_ref[0, 0, pl.ds(ks, tk), :], aug_sc[...])
        if masked:
            st = jnp.where(ks + lax.broadcasted_iota(jnp.int32, (tk, 1), 0) <= qpos, st, NEG)
        _flash_step(st, v_ref[0, 0, :, pl.ds(ks, tk)], m_sc, l_sc, acc_sc)

    _causal_sweep(qstart, tq, tk, chunk)
    o = acc_sc[...] / l_sc[...]
    for r in range(R):
        o_ref[0, 0, r] = o[:, r * tq:(r + 1) * tq].T


def _nsa_slc(qt, k, vt, mneg, tq=256, tk=512):
    B, G, R, dh, S = qt.shape
    nsb = mneg.shape[2]
    SLC_KA = k.shape[-1]
    assert SLC_KA == _slc_aug_width(nsb)
    tq = min(tq, S)
    tk = min(tk, S)
    return pl.pallas_call(
        functools.partial(_slc_kernel, tq=tq, tk=tk, nsb=nsb),
        grid=(B, G, S // tq),
        in_specs=[pl.BlockSpec((1, 1, R, dh, tq), lambda b, g, i: (b, g, 0, 0, i)),
                  pl.BlockSpec((1, 1, S, SLC_KA), lambda b, g, i: (b, g, 0, 0)),
                  pl.BlockSpec((1, 1, dh, S), lambda b, g, i: (b, g, 0, 0)),
                  pl.BlockSpec((1, 1, nsb, tq), lambda b, g, i: (b, g, 0, i))],
        out_specs=pl.BlockSpec((1, 1, R, tq, dh), lambda b, g, i: (b, g, 0, i, 0)),
        out_shape=jax.ShapeDtypeStruct((B, G, R, S, dh), F32),
        scratch_shapes=[pltpu.VMEM((SLC_KA, R * tq), BF16), pltpu.VMEM((1, R * tq), F32),
                        pltpu.VMEM((1, R * tq), F32), pltpu.VMEM((dh, R * tq), F32)],
        compiler_params=_params(("parallel", "parallel", "arbitrary")),
        name="nsa_slc",
    )(qt, k, vt, mneg)


def _win_kernel(q_ref, k_ref, v_ref, o_ref, *, tq):
    g = pl.program_id(1)
    qi = pl.program_id(2)
    qstart = qi * tq
    R = NSA_REP
    span = tq + NSA_WINDOW
    q = q_ref[0, 0].reshape(R * tq, NSA_DIM)
    ks = pl.multiple_of(qstart, tq)
    s_all = _dot_nt(q, k_ref[0, 0, pl.ds(ks, span), :])
    kpos = qstart - NSA_WINDOW + lax.broadcasted_iota(jnp.int32, (1, span), 1)
    qpos = qstart + lax.broadcasted_iota(jnp.int32, (tq, 1), 0)
    dist = qpos - kpos
    valid = (dist >= 0) & (dist < NSA_WINDOW) & (kpos >= 0)
    distf = dist.astype(F32)
    parts = []
    for r in range(R):
        s = s_all[r * tq:(r + 1) * tq] - _nsa_slope(g, jnp.int32(r)) * distf
        s = jnp.where(valid, s, NEG)
        e = jnp.exp(s - jnp.max(s, axis=-1, keepdims=True))
        parts.append(e / jnp.sum(e, axis=-1, keepdims=True))
    p = jnp.concatenate(parts, axis=0).astype(BF16)
    o_ref[0, 0] = _dot(p, v_ref[0, 0, pl.ds(ks, span), :]).reshape(R, tq, NSA_DIM)


def _nsa_win(q, kpad, vpad, tq=256):
    B, G, R, S, dh = q.shape
    SP = kpad.shape[2]
    tq = min(tq, S)
    return pl.pallas_call(
        functools.partial(_win_kernel, tq=tq),
        grid=(B, G, S // tq),
        in_specs=[pl.BlockSpec((1, 1, R, tq, dh), lambda b, g, i: (b, g, 0, i, 0)),
                  pl.BlockSpec((1, 1, SP, dh), lambda b, g, i: (b, g, 0, 0)),
                  pl.BlockSpec((1, 1, SP, dh), lambda b, g, i: (b, g, 0, 0))],
        out_specs=pl.BlockSpec((1, 1, R, tq, dh), lambda b, g, i: (b, g, 0, i, 0)),
        out_shape=jax.ShapeDtypeStruct((B, G, R, S, dh), F32),
        compiler_params=_params(("parallel", "parallel", "parallel")),
        name="nsa_win",
    )(q, kpad, vpad)


def _mix_out_kernel(x_ref, oa_ref, oc_ref, os_ref, ow_ref, pg_ref, wa_ref, wb_ref, wo_ref, h_ref):
    D = D_MODEL
    nq = NSA_Q
    gl = 2 * D
    o_b = (_sigmoid(pg_ref[:, gl:gl + nq]) * oc_ref[...]
           + _sigmoid(pg_ref[:, gl + nq:gl + 2 * nq]) * os_ref[...]
           + _sigmoid(pg_ref[:, gl + 2 * nq:gl + 3 * nq]) * ow_ref[...])
    merged = (_sigmoid(pg_ref[:, 0:D]) * _dot(oa_ref[...], wa_ref[...])
              + _sigmoid(pg_ref[:, D:2 * D]) * _dot(o_b.astype(BF16), wb_ref[...]))
    h_ref[...] = x_ref[...] + _dot(merged.astype(BF16), wo_ref[...])


def _mix_out(x2, oa, oc, osl, ow, pg, wa, wb, wo, tm=256):
    T, D = x2.shape
    row = lambda i: (i, 0)
    const = lambda i: (0, 0)
    return pl.pallas_call(
        _mix_out_kernel,
        grid=(T // tm,),
        in_specs=[pl.BlockSpec((tm, D), row), pl.BlockSpec((tm, DA_V), row),
                  pl.BlockSpec((tm, NSA_Q), row), pl.BlockSpec((tm, NSA_Q), row),
                  pl.BlockSpec((tm, NSA_Q), row), pl.BlockSpec((tm, NG_COLS), row),
                  pl.BlockSpec((DA_V, D), const), pl.BlockSpec((NSA_Q, D), const),
                  pl.BlockSpec((D, D), const)],
        out_specs=pl.BlockSpec((tm, D), row),
        out_shape=jax.ShapeDtypeStruct((T, D), F32),
        compiler_params=_params(("parallel",)),
        name="mix_out",
    )(x2, oa, oc, osl, ow, pg, wa, wb, wo)


def _mem_kv_kernel(m_ref, g_ref, w_ref, o_ref):
    o_ref[...] = _dot(_rms(m_ref[...], g_ref[...]).astype(BF16), w_ref[...]).astype(o_ref.dtype)


def _mem_kv(mem2, g, w):
    M, D = mem2.shape
    N = w.shape[1]
    return pl.pallas_call(
        _mem_kv_kernel,
        grid=(1,),
        in_specs=[pl.BlockSpec((M, D), lambda i: (0, 0)), pl.BlockSpec((1, D), lambda i: (0, 0)),
                  pl.BlockSpec((D, N), lambda i: (0, 0))],
        out_specs=pl.BlockSpec((M, N), lambda i: (0, 0)),
        out_shape=jax.ShapeDtypeStruct((M, N), BF16),
        compiler_params=_params(("arbitrary",)),
        name="mem_kv",
    )(mem2, g, w)


def _cross_kernel(h_ref, g_ref, wq_ref, kv_ref, wo_ref, o_ref):
    h = h_ref[0]
    u = _rms(h, g_ref[...]).astype(BF16)
    q = (_dot(u, wq_ref[...]) * np.float32(CA_DIM ** -0.5)).astype(BF16)
    hd = CA_HEADS * CA_DIM
    outs = []
    for a in range(CA_HEADS):
        kh = kv_ref[0, :, a * CA_DIM:(a + 1) * CA_DIM]
        vh = kv_ref[0, :, hd + a * CA_DIM:hd + (a + 1) * CA_DIM]
        s = _dot_nt(q[:, a * CA_DIM:(a + 1) * CA_DIM], kh)
        e = jnp.exp(s - jnp.max(s, axis=-1, keepdims=True))
        p = e / jnp.sum(e, axis=-1, keepdims=True)
        outs.append(_dot(p.astype(BF16), vh))
    o = jnp.concatenate(outs, axis=-1).astype(BF16)
    o_ref[0] = h + _dot(o, wo_ref[...])


def _cross_attn(h, g, wq, kv, wo, tm=256):
    B, S, D = h.shape
    M = kv.shape[1]
    hd = CA_HEADS * CA_DIM
    return pl.pallas_call(
        _cross_kernel,
        grid=(B, S // tm),
        in_specs=[pl.BlockSpec((1, tm, D), lambda b, i: (b, i, 0)),
                  pl.BlockSpec((1, D), lambda b, i: (0, 0)),
                  pl.BlockSpec((D, hd), lambda b, i: (0, 0)),
                  pl.BlockSpec((1, M, 2 * hd), lambda b, i: (b, 0, 0)),
                  pl.BlockSpec((hd, D), lambda b, i: (0, 0))],
        out_specs=pl.BlockSpec((1, tm, D), lambda b, i: (b, i, 0)),
        out_shape=jax.ShapeDtypeStruct((B, S, D), F32),
        compiler_params=_params(("parallel", "parallel")),
        name="cross_attn",
    )(h, g, wq, kv, wo)


def _peer_query_kernel(h_ref, g_ref, wqt_ref, sk_ref, u_ref, s_ref):
    ut = _rms(h_ref[...], g_ref[...]).T.astype(BF16)
    u_ref[...] = ut
    half = PEER_DKEY // 2
    qt = _dot(wqt_ref[...], ut).astype(BF16)
    for hc in range(2 * PEER_HEADS):
        s_ref[hc] = _dot(sk_ref[hc], qt[hc * half:(hc + 1) * half])


def _peer_query(h2, g, wqt, sk, tm=512):
    T, D = h2.shape
    HC = sk.shape[0]
    tm = min(tm, T)
    return pl.pallas_call(
        _peer_query_kernel,
        grid=(T // tm,),
        in_specs=[pl.BlockSpec((tm, D), lambda i: (i, 0)), pl.BlockSpec((1, D), lambda i: (0, 0)),
                  pl.BlockSpec(wqt.shape, lambda i: (0, 0)),
                  pl.BlockSpec(sk.shape, lambda i: (0, 0, 0))],
        out_specs=[pl.BlockSpec((D, tm), lambda i: (0, i)),
                   pl.BlockSpec((HC, PEER_NKEYS, tm), lambda i: (0, 0, i))],
        out_shape=[jax.ShapeDtypeStruct((D, T), BF16),
                   jax.ShapeDtypeStruct((HC, PEER_NKEYS, T), F32)],
        compiler_params=_params(("parallel",)),
        name="peer_query",
    )(h2, g, wqt, sk)


def _top_rows(x, k):
    n = x.shape[0]
    row = lax.broadcasted_iota(jnp.int32, x.shape, 0)
    rank = jnp.full(x.shape, float(k), F32)
    vals = []
    for i in range(k):
        mx = jnp.max(x, axis=0, keepdims=True)
        idx = jnp.min(jnp.where(x == mx, row, n), axis=0, keepdims=True)
        hit = row == idx
        rank = jnp.where(hit, float(i), rank)
        x = jnp.where(hit, -jnp.inf, x)
        vals.append(mx)
    return vals, rank


def _peer_route_kernel(s_ref, cnt_ref, a_ref, rank_ref, b_ref):
    K = PEER_TOPK
    s0 = s_ref[0]
    s1 = s_ref[1]
    v0, rank0 = _top_rows(s0, K)
    v1, rank1 = _top_rows(s1, K)
    top0 = jnp.concatenate(v0, axis=0)
    top1 = jnp.concatenate(v1, axis=0)
    sub = lax.broadcasted_iota(jnp.int32, (8, 1), 0)
    pieces = [v0[0] + top1, v0[1] + top1[0:8]]
    for k0 in range(2, 8):
        pieces.append(jnp.where(sub < K // (k0 + 1), v0[k0] + top1[0:8], -jnp.inf))
    pieces.append(top0[8:16] + v1[0])
    cand = jnp.concatenate(pieces, axis=0)
    cvals, crank = _top_rows(cand, K)
    taken = (crank < K).astype(F32)
    z = jnp.exp(cvals[0] - cvals[0])
    for i in range(1, K):
        z = z + jnp.exp(cvals[i] - cvals[0])
    cnt0 = jnp.zeros(s0.shape, F32)
    for k0 in range(8):
        lo = 0 if k0 == 0 else 8 + 8 * k0
        hi = 16 if k0 == 0 else lo + 8
        cnt = jnp.sum(taken[lo:hi], axis=0, keepdims=True)
        cnt0 = jnp.where(rank0 == float(k0), cnt, cnt0)
    for k0 in range(8, K):
        cnt0 = jnp.where(rank0 == float(k0), taken[72 + k0 - 8:73 + k0 - 8], cnt0)
    cnt_ref[0] = cnt0
    a_ref[0] = jnp.where(rank0 < K, 0.5 * jnp.exp(s0 - v0[0]) / z, 0.0)
    rank_ref[0] = rank1.astype(rank_ref.dtype)
    b_ref[0] = jnp.where(rank1 < K, jnp.exp(s1 - v1[0]), 0.0).astype(b_ref.dtype)


def _peer_route(st, tl=256):
    HC, NK, T = st.shape
    H = HC // 2
    tl = min(tl, T)
    spec = pl.BlockSpec((1, NK, tl), lambda h, i: (h, 0, i))
    f32 = jax.ShapeDtypeStruct((H, NK, T), F32)
    bf16 = jax.ShapeDtypeStruct((H, NK, T), BF16)
    return pl.pallas_call(
        _peer_route_kernel,
        grid=(H, T // tl),
        in_specs=[pl.BlockSpec((2, NK, tl), lambda h, i: (h, 0, i))],
        out_specs=[spec, spec, spec, spec],
        out_shape=[f32, f32, bf16, bf16],
        compiler_params=_params(("parallel", "parallel")),
        name="peer_route",
    )(st)


def _peer_expert_kernel(u_ref, ut_ref, vt_ref, cnt_ref, a_ref, rank_ref, b_ref, o_ref, *, rows):
    e = pl.program_id(1)
    nk = PEER_NKEYS
    tt = u_ref.shape[1]

    @pl.when(e == 0)
    def _():
        o_ref[...] = jnp.zeros(o_ref.shape, F32)

    x = _dot(ut_ref[...], u_ref[...]).astype(BF16)
    act = x * (1.0 + lax.erf(x * float(1.0 / np.sqrt(2.0))))
    parts = []
    for ii in range(rows):
        gate = jnp.zeros((nk, tt), BF16)
        for h in range(PEER_HEADS):
            cnt = jnp.broadcast_to(cnt_ref[ii, h:h + 1, :], (nk, tt)).astype(BF16)
            a = jnp.broadcast_to(a_ref[ii, h:h + 1, :], (nk, tt)).astype(BF16)
            gate = gate + jnp.where(rank_ref[h] < cnt, b_ref[h] * a, 0.0)
        parts.append(gate * act[ii * nk:(ii + 1) * nk])
    w = jnp.concatenate(parts, axis=0)
    o_ref[...] += _dot(vt_ref[...], w)


def _peer_expert(u3t, ut, vt, cnt, a, rank, b, tt=512, rows=8):
    D, T = u3t.shape
    NE = ut.shape[0]
    H, NK, _ = rank.shape
    te = rows * NK
    tt = min(tt, T)
    ispec = pl.BlockSpec((rows, H, tt), lambda t, e: (e, 0, t))
    jspec = pl.BlockSpec((H, NK, tt), lambda t, e: (0, 0, t))
    return pl.pallas_call(
        functools.partial(_peer_expert_kernel, rows=rows),
        grid=(T // tt, NE // te),
        in_specs=[pl.BlockSpec((D, tt), lambda t, e: (0, t)),
                  pl.BlockSpec((te, D), lambda t, e: (e, 0)),
                  pl.BlockSpec((D, te), lambda t, e: (0, e)),
                  ispec, ispec, jspec, jspec],
        out_specs=pl.BlockSpec((D, tt), lambda t, e: (0, t)),
        out_shape=jax.ShapeDtypeStruct((D, T), F32),
        compiler_params=_params(("parallel", "arbitrary")),
        name="peer_expert",
    )(u3t, ut, vt, cnt, a, rank, b)


def _final_kernel(h_ref, pt_ref, g_ref, o_ref):
    o_ref[...] = _rms(h_ref[...] + pt_ref[...].T, g_ref[...])


def _final_norm(h2, pt, g, tm=256):
    T, D = h2.shape
    return pl.pallas_call(
        _final_kernel,
        grid=(T // tm,),
        in_specs=[pl.BlockSpec((tm, D), lambda i: (i, 0)), pl.BlockSpec((D, tm), lambda i: (0, i)),
                  pl.BlockSpec((1, D), lambda i: (0, 0))],
        out_specs=pl.BlockSpec((tm, D), lambda i: (i, 0)),
        out_shape=jax.ShapeDtypeStruct((T, D), F32),
        compiler_params=_params(("parallel",)),
        name="final_norm",
    )(h2, pt, g)


def _regroup_w_in(w):
    qa, ka, va, qb, kvb, gb, ga, gbm = jnp.split(w, np.cumsum(IN_SIZES)[:-1].tolist(), axis=-1)
    gbe = gb.reshape(-1, NSA_HEADS, 3).transpose(0, 2, 1)
    gbe = jnp.repeat(gbe, NSA_DIM, axis=-1)
    gbe = gbe.reshape(w.shape[0], 3 * NSA_Q)
    return jnp.concatenate([qa * DA_DIM ** -0.5, ka, va, qb * NSA_DIM ** -0.5, kvb, ga, gbm, gbe],
                           axis=-1).astype(BF16)


def _layer(h, mem, l, p):
    B, S, D = h.shape
    T = B * S
    G, R, dh = NSA_GROUPS, NSA_REP, NSA_DIM
    lambda_init = 0.8 - 0.6 * float(np.exp(-0.3 * l))
    x2 = h.reshape(T, D)

    pb, pg = _in_proj(x2, p["norm_mix"][l][None], _regroup_w_in(p["w_in"][l]))
    pb = pb.reshape(B, S, NB_COLS)
    c = 0
    qa = pb[..., c:c + DA_QK].reshape(B, S, DA_HEADS, 2, DA_DIM).transpose(0, 2, 3, 4, 1); c += DA_QK
    ka = pb[..., c:c + DA_QK].reshape(B, S, DA_HEADS, 2, DA_DIM).transpose(0, 2, 3, 1, 4); c += DA_QK
    va = pb[..., c:c + DA_V].reshape(B, S, DA_HEADS, 2 * DA_DIM).transpose(0, 2, 3, 1); c += DA_V
    qb5 = pb[..., c:c + NSA_Q].reshape(B, S, G, R, dh); c += NSA_Q
    qb = qb5.transpose(0, 2, 3, 1, 4)
    qbt = qb5.transpose(0, 2, 3, 4, 1)
    kvb = pb[..., c:c + NSA_KV].reshape(B, S, 6, G, dh).transpose(2, 0, 3, 1, 4)

    kp = _kpos_cols(S)
    npad = 14
    da_slope = jnp.exp2(-2.0 * jnp.arange(1, DA_HEADS + 1, dtype=F32)).astype(BF16)
    q_rows = jnp.concatenate([jnp.broadcast_to(da_slope[None, :, None, None, None], (B, DA_HEADS, 2, 2, S)),
                              jnp.zeros((B, DA_HEADS, 2, npad, S), BF16)], axis=3)
    k_cols = jnp.concatenate([jnp.broadcast_to(kp, (B, DA_HEADS, 2, S, 2)),
                              jnp.zeros((B, DA_HEADS, 2, S, npad), BF16)], axis=-1)
    o_a = _diff_attn(jnp.concatenate([qa, q_rows], axis=3), jnp.concatenate([ka, k_cols], axis=-1), va,
                     p["diff_lambda"][l], p["diff_subln"][l][:, None], lambda_init)

    nc = S // NSA_CMP_BLOCK
    kdim = NSA_CMP_BLOCK * dh
    xkv = kvb[0:2].reshape(2, B * G * nc, kdim)
    w1 = jnp.stack([p["nsa_ck_w1"][l], p["nsa_cv_w1"][l]]).astype(BF16)
    w2 = jnp.stack([p["nsa_ck_w2"][l], p["nsa_cv_w2"][l]]).astype(BF16)
    kvc = _compress(xkv, p["nsa_cmp_pos"][l].reshape(1, kdim), w1, w2)
    kvc = kvc.reshape(2, B, G, nc // 2, 2, dh).transpose(0, 1, 2, 4, 3, 5).reshape(2, B, G, nc, dh)
    o_cmp, mneg = _nsa_cmp(qb, kvc[0], kvc[1])
    nsb = S // NSA_SEL_BLOCK
    onehot = (jnp.arange(S, dtype=jnp.int32)[:, None] // NSA_SEL_BLOCK
              == jnp.arange(nsb, dtype=jnp.int32)[None, :]).astype(BF16)
    k_aug = jnp.concatenate([onehot, kp, jnp.zeros((S, _slc_aug_width(nsb) - dh - nsb - 2), BF16)], axis=-1)
    k_aug = jnp.concatenate([kvb[2], jnp.broadcast_to(k_aug, (B, G) + k_aug.shape)], axis=-1)
    o_slc = _nsa_slc(qbt, k_aug, kvb[3].transpose(0, 1, 3, 2), mneg)
    pad = ((0, 0), (0, 0), (NSA_WINDOW, 0), (0, 0))
    o_win = _nsa_win(qb, jnp.pad(kvb[4], pad), jnp.pad(kvb[5], pad))

    def tok_major(o):
        return o.transpose(0, 3, 1, 2, 4).reshape(T, NSA_Q)

    h1 = _mix_out(x2, o_a.reshape(T, DA_V), tok_major(o_cmp), tok_major(o_slc), tok_major(o_win), pg,
                  p["w_branch_a"][l].astype(BF16), p["w_branch_b"][l].astype(BF16),
                  p["w_out"][l].astype(BF16))

    M = mem.shape[1]
    kvm = _mem_kv(mem.reshape(B * M, D), p["norm_mem"][l][None], p["w_cross_kv"][l].astype(BF16))
    h2 = _cross_attn(h1.reshape(B, S, D), p["norm_cross"][l][None], p["w_cross_q"][l].astype(BF16),
                     kvm.reshape(B, M, -1), p["w_cross_o"][l].astype(BF16)).reshape(T, D)

    sk = p["peer_subkeys"][l].reshape(2 * PEER_HEADS, PEER_NKEYS, PEER_DKEY // 2).astype(BF16)
    u3, st = _peer_query(h2, p["norm_ffn"][l][None], p["peer_wq"][l].T.astype(BF16), sk)
    cnt, a, rank, b = _peer_route(st)
    pt = _peer_expert(u3, p["peer_u"][l].astype(BF16), p["peer_v"][l].T.astype(BF16),
                      cnt.transpose(1, 0, 2), a.transpose(1, 0, 2), rank, b)
    return h2, pt


def kernel(x, mem, norm_mix, w_in, diff_lambda, diff_subln, nsa_cmp_pos, nsa_ck_w1, nsa_ck_w2,
           nsa_cv_w1, nsa_cv_w2, w_branch_a, w_branch_b, w_out, norm_cross, norm_mem,
           w_cross_q, w_cross_kv, w_cross_o, norm_ffn, peer_wq, peer_subkeys, peer_u, peer_v,
           norm_final):
    p = dict(norm_mix=norm_mix, w_in=w_in, diff_lambda=diff_lambda, diff_subln=diff_subln,
             nsa_cmp_pos=nsa_cmp_pos, nsa_ck_w1=nsa_ck_w1, nsa_ck_w2=nsa_ck_w2, nsa_cv_w1=nsa_cv_w1,
             nsa_cv_w2=nsa_cv_w2, w_branch_a=w_branch_a, w_branch_b=w_branch_b, w_out=w_out,
             norm_cross=norm_cross, norm_mem=norm_mem, w_cross_q=w_cross_q, w_cross_kv=w_cross_kv,
             w_cross_o=w_cross_o, norm_ffn=norm_ffn, peer_wq=peer_wq, peer_subkeys=peer_subkeys,
             peer_u=peer_u, peer_v=peer_v)
    B, S, D = x.shape
    depth = w_in.shape[0]
    assert depth == 1, "the last layer's PEER output is folded into the final norm"
    h2, pt = _layer(x, mem, 0, p)
    return _final_norm(h2, pt, norm_final[None]).reshape(B, S, D)
```

```python
import functools

import jax
import jax.numpy as jnp
import numpy as np
from jax import lax
from jax.experimental import pallas as pl
from jax.experimental.pallas import tpu as pltpu

F32 = jnp.float32
BF16 = jnp.bfloat16

D_MODEL = 1024
DA_HEADS = 4
DA_DIM = 64
NSA_HEADS = 8
NSA_GROUPS = 2
NSA_REP = NSA_HEADS // NSA_GROUPS
NSA_DIM = 64
NSA_CMP_BLOCK = 32
NSA_SEL_BLOCK = 64
NSA_TOPN = 16
NSA_WINDOW = 512
NSA_CMP_HIDDEN = 256
CA_HEADS = 4
CA_DIM = 128
PEER_HEADS = 8
PEER_NKEYS = 128
PEER_DKEY = 256
PEER_TOPK = 16

DA_QK = DA_HEADS * 2 * DA_DIM
DA_V = DA_HEADS * 2 * DA_DIM
NSA_Q = NSA_HEADS * NSA_DIM
NSA_KV = 6 * NSA_GROUPS * NSA_DIM
NSA_G = NSA_HEADS * 3
IN_SIZES = (DA_QK, DA_QK, DA_V, NSA_Q, NSA_KV, NSA_G, D_MODEL, D_MODEL)

EPS = 1e-6
LOG2E = float(np.log2(np.e))
NEG = -1e30
V7X_VMEM_LIMIT = 48 * 1024 * 1024

NB_COLS = DA_QK * 2 + DA_V + NSA_Q + NSA_KV
NG_COLS = 2 * D_MODEL + 3 * NSA_Q


def _params(sem, flags=None):
    return pltpu.CompilerParams(dimension_semantics=sem, vmem_limit_bytes=V7X_VMEM_LIMIT, flags=flags)


def _rms(x, g):
    return x * lax.rsqrt(jnp.mean(x * x, axis=-1, keepdims=True) + EPS) * g


def _dot_nt(a, b):
    return lax.dot_general(a, b, (((1,), (1,)), ((), ())), preferred_element_type=F32)


def _dot(a, b):
    return jnp.dot(a, b, preferred_element_type=F32)


def _gelu(x):
    return 0.5 * x * (1.0 + lax.erf(x * np.float32(1.0 / np.sqrt(2.0))))


def _sigmoid(x):
    return 1.0 / (1.0 + jnp.exp(-x))


def _in_proj_kernel(x_ref, g_ref, w_ref, ob_ref, og_ref, *, chunk):
    u = _rms(x_ref[...], g_ref[...]).astype(BF16)
    for c0 in range(0, NB_COLS, chunk):
        c1 = min(c0 + chunk, NB_COLS)
        ob_ref[:, c0:c1] = _dot(u, w_ref[:, c0:c1]).astype(BF16)
    for c0 in range(0, NG_COLS, chunk):
        c1 = min(c0 + chunk, NG_COLS)
        og_ref[:, c0:c1] = _dot(u, w_ref[:, NB_COLS + c0:NB_COLS + c1])


def _in_proj(x2, g, w, tm=512):
    T, D = x2.shape
    N = w.shape[1]
    return pl.pallas_call(
        functools.partial(_in_proj_kernel, chunk=512),
        grid=(T // tm,),
        in_specs=[pl.BlockSpec((tm, D), lambda i: (i, 0)),
                  pl.BlockSpec((1, D), lambda i: (0, 0)),
                  pl.BlockSpec((D, N), lambda i: (0, 0))],
        out_specs=[pl.BlockSpec((tm, NB_COLS), lambda i: (i, 0)),
                   pl.BlockSpec((tm, NG_COLS), lambda i: (i, 0))],
        out_shape=[jax.ShapeDtypeStruct((T, NB_COLS), BF16),
                   jax.ShapeDtypeStruct((T, NG_COLS), F32)],
        compiler_params=_params(("parallel",)),
        name="in_proj",
    )(x2, g, w)


def _flash_step(st, vt, m_ref, l_ref, acc_ref):
    m_old = m_ref[...]
    m_new = jnp.maximum(m_old, jnp.max(st, axis=0, keepdims=True))
    p = jnp.exp2(st - m_new)
    alpha = jnp.exp2(m_old - m_new)
    l_ref[...] = alpha * l_ref[...] + jnp.sum(p, axis=0, keepdims=True)
    acc_ref[...] = alpha * acc_ref[...] + _dot(vt, p.astype(BF16))
    m_ref[...] = m_new


def _pipelined_sweep(n_full, scores, consume, buf_a, buf_b):
    scores(0, buf_a)

    def body(i, carry):
        j = 2 * i
        scores(j + 1, buf_b)
        consume(j, buf_a, False)
        scores(j + 2, buf_a)
        consume(j + 1, buf_b, False)
        return carry

    lax.fori_loop(0, n_full // 2, body, 0)
    odd = n_full % 2 == 1

    @pl.when(odd)
    def _():
        scores(n_full, buf_b)
        consume(n_full - 1, buf_a, False)
        consume(n_full, buf_b, True)

    @pl.when(jnp.logical_not(odd))
    def _():
        consume(n_full, buf_a, True)


def _kpos_cols(S):
    kpos = jnp.arange(S, dtype=jnp.int32)
    return jnp.stack([(kpos // 256) * 256, kpos % 256], axis=-1).astype(BF16)


def _diff_attn_kernel(lam_ref, sub_ref, q_ref, k_ref, v_ref, o_ref, m_sc, l_sc, acc_sc, sa_sc, sb_sc,
                      *, tq, tk, lambda_init):
    qi = pl.program_id(2)
    qstart = qi * tq
    m_sc[...] = jnp.full(m_sc.shape, NEG, F32)
    l_sc[...] = jnp.zeros(l_sc.shape, F32)
    acc_sc[...] = jnp.zeros(acc_sc.shape, F32)
    qpos = qstart + lax.broadcasted_iota(jnp.int32, (1, tq), 1)

    def scores(j, buf):
        ks = pl.multiple_of(j * tk, tk)
        for c in range(2):
            buf[c] = _dot(k_ref[0, 0, c, pl.ds(ks, tk), :], q_ref[0, 0, c])

    def consume(j, buf, masked):
        ks = pl.multiple_of(j * tk, tk)
        vt = v_ref[0, 0, :, pl.ds(ks, tk)]
        if masked:
            keep = ks + lax.broadcasted_iota(jnp.int32, (tk, 1), 0) <= qpos
        for c in range(2):
            st = jnp.where(keep, buf[c], NEG) if masked else buf[c]
            _flash_step(st, vt, m_sc.at[c], l_sc.at[c], acc_sc.at[c])

    _pipelined_sweep(qstart // tk, scores, consume, sa_sc, sb_sc)

    lp = lam_ref[...]
    lam = (jnp.exp(jnp.sum(lp[0:1] * lp[1:2], axis=-1, keepdims=True))
           - jnp.exp(jnp.sum(lp[2:3] * lp[3:4], axis=-1, keepdims=True)) + lambda_init)
    o = acc_sc[0] / l_sc[0] - lam * (acc_sc[1] / l_sc[1])
    y = o * lax.rsqrt(jnp.mean(o * o, axis=0, keepdims=True) + EPS) * sub_ref[...]
    o_ref[0] = (y * (1.0 - lambda_init)).T.astype(o_ref.dtype)


def _diff_attn(qt, k, vt, lam_params, subln, lambda_init, tq=512, tk=512):
    B, H, _, da, S = qt.shape
    dv = vt.shape[2]
    tq = min(tq, S)
    tk = min(tk, S)
    assert tk % tq == 0, "one masked diagonal key chunk per query tile"
    return pl.pallas_call(
        functools.partial(_diff_attn_kernel, tq=tq, tk=tk, lambda_init=lambda_init),
        grid=(B, H, S // tq),
        in_specs=[pl.BlockSpec(lam_params.shape, lambda b, h, i: (0, 0)),
                  pl.BlockSpec((dv, 1), lambda b, h, i: (0, 0)),
                  pl.BlockSpec((1, 1, 2, da, tq), lambda b, h, i: (b, h, 0, 0, i)),
                  pl.BlockSpec((1, 1, 2, S, da), lambda b, h, i: (b, h, 0, 0, 0)),
                  pl.BlockSpec((1, 1, dv, S), lambda b, h, i: (b, h, 0, 0))],
        out_specs=pl.BlockSpec((1, tq, dv), lambda b, h, i: (b, i, h)),
        out_shape=jax.ShapeDtypeStruct((B, S, H * dv), BF16),
        scratch_shapes=[pltpu.VMEM((2, 1, tq), F32), pltpu.VMEM((2, 1, tq), F32),
                        pltpu.VMEM((2, dv, tq), F32),
                        pltpu.VMEM((2, tk, tq), F32), pltpu.VMEM((2, tk, tq), F32)],
        compiler_params=_params(("parallel", "parallel", "arbitrary")),
        name="diff_attn",
    )(lam_params, subln, qt, k, vt)


def _compress_kernel(x_ref, pos_ref, w1_ref, w2_ref, o_ref):
    tb = (x_ref[0].astype(F32) + pos_ref[...]).astype(BF16)
    hid = _gelu(_dot(tb, w1_ref[0]))
    o_ref[0] = _dot(hid.astype(BF16), w2_ref[0]).astype(o_ref.dtype)


def _compress(xkv, pos, w1, w2):
    _, rows, kdim = xkv.shape
    hid = w1.shape[-1]
    dh = w2.shape[-1]
    return pl.pallas_call(
        _compress_kernel,
        grid=(2,),
        in_specs=[pl.BlockSpec((1, rows, kdim), lambda j: (j, 0, 0)),
                  pl.BlockSpec((1, kdim), lambda j: (0, 0)),
                  pl.BlockSpec((1, kdim, hid), lambda j: (j, 0, 0)),
                  pl.BlockSpec((1, hid, dh), lambda j: (j, 0, 0))],
        out_specs=pl.BlockSpec((1, rows, dh), lambda j: (j, 0, 0)),
        out_shape=jax.ShapeDtypeStruct((2, rows, dh), BF16),
        compiler_params=_params(("parallel",)),
        name="nsa_compress",
    )(xkv, pos, w1, w2)


def _nsa_slope(g, r):
    return jnp.exp2(-(4 * g + r + 1).astype(F32)) * LOG2E


def _split3(v):
    c1 = v.astype(BF16)
    r1 = v - c1.astype(F32)
    c2 = r1.astype(BF16)
    c3 = (r1 - c2.astype(F32)).astype(BF16)
    return c1, c2, c3


def _cmp_kernel(q_ref, kc_ref, vct_ref, o_ref, mneg_ref, *, tq, nc):
    g = pl.program_id(1)
    qi = pl.program_id(2)
    nsb = nc // 2
    R = NSA_REP
    n = R * tq
    qt = jnp.concatenate([q_ref[0, 0, r] for r in range(R)], axis=1)
    st = _dot(kc_ref[0, 0], qt)
    row = lax.broadcasted_iota(jnp.int32, (nc, 1), 0)
    cblk = jnp.where(row < nsb, 2 * row, 2 * (row - nsb) + 1)
    cpos = cblk * NSA_CMP_BLOCK + (NSA_CMP_BLOCK - 1)
    col = lax.broadcasted_iota(jnp.int32, (1, n), 1)
    dist = (qi * tq + col % tq - cpos).astype(F32)
    valid = dist >= 0
    s = jnp.where(valid, st - _nsa_slope(g, col // tq) * dist, NEG)
    e = jnp.exp2(s - jnp.max(s, axis=0, keepdims=True))
    p = jnp.where(valid, e / jnp.sum(e, axis=0, keepdims=True), 0.0)
    ot = _dot(vct_ref[0, 0], p.astype(BF16))
    o_ref[0] = jnp.concatenate([ot[:, r * tq:(r + 1) * tq] for r in range(R)], axis=0).T

    pp = p[:nsb] + p[nsb:]
    p_slc = pp[:, 0:tq]
    for r in range(1, R):
        p_slc = p_slc + pp[:, r * tq:(r + 1) * tq]
    blk = lax.broadcasted_iota(jnp.int32, (nsb, 1), 0)
    cur = (qi * tq + lax.broadcasted_iota(jnp.int32, (1, tq), 1)) // NSA_SEL_BLOCK
    forced = (blk == 0) | (blk == cur) | (blk == cur - 1)
    score = jnp.where(forced, 1e4, jnp.where(blk <= cur, p_slc, -1e9))
    n_sel = min(NSA_TOPN, nsb)
    _, rank = _top_rows(score, n_sel)
    mneg_ref[0, 0] = jnp.where(rank < n_sel, 0.0, NEG).astype(mneg_ref.dtype)


def _nsa_cmp(qt, kc, vct, tq=256):
    B, G, R, dh, S = qt.shape
    nc = kc.shape[2]
    tq = min(tq, S)
    return pl.pallas_call(
        functools.partial(_cmp_kernel, tq=tq, nc=nc),
        grid=(B, G, S // tq),
        in_specs=[pl.BlockSpec((1, 1, R, dh, tq), lambda b, g, i: (b, g, 0, 0, i)),
                  pl.BlockSpec((1, 1, nc, dh), lambda b, g, i: (b, g, 0, 0)),
                  pl.BlockSpec((1, 1, dh, nc), lambda b, g, i: (b, g, 0, 0))],
        out_specs=[pl.BlockSpec((1, tq, R * dh), lambda b, g, i: (b, i, g)),
                   pl.BlockSpec((1, 1, nc // 2, tq), lambda b, g, i: (b, g, 0, i))],
        out_shape=[jax.ShapeDtypeStruct((B, S, G * R * dh), F32),
                   jax.ShapeDtypeStruct((B, G, nc // 2, S), BF16)],
        compiler_params=_params(("parallel", "parallel", "parallel")),
        name="nsa_cmp",
    )(qt, kc, vct)


def _slc_aug_width(nsb):
    return -(-(NSA_DIM + nsb + 16) // 128) * 128


def _slc_kernel(q_ref, k_ref, v_ref, mneg_ref, o_ref, aug_sc, m_sc, l_sc, acc_sc, sa_sc, sb_sc,
                *, tq, tk, nsb):
    g = pl.program_id(1)
    qi = pl.program_id(2)
    qstart = qi * tq
    R = NSA_REP
    n = R * tq
    bias_row = NSA_DIM + nsb
    aug_sc[0:NSA_DIM, :] = jnp.concatenate([q_ref[0, 0, r] for r in range(R)], axis=1)
    aug_sc[NSA_DIM:bias_row, :] = jnp.concatenate([mneg_ref[0, 0]] * R, axis=1)
    col = lax.broadcasted_iota(jnp.int32, (16, n), 1)
    row = lax.broadcasted_iota(jnp.int32, (16, n), 0)
    c1, c2, c3 = (c.astype(F32) for c in _split3(_nsa_slope(g, col // tq)))
    aug_sc[bias_row:bias_row + 16, :] = jnp.where(
        row < 2, c1, jnp.where(row < 4, c2, jnp.where(row < 6, c3, 0.0))).astype(BF16)
    aug_sc[bias_row + 16:, :] = jnp.zeros((aug_sc.shape[0] - bias_row - 16, n), BF16)
    m_sc[...] = jnp.full(m_sc.shape, NEG, F32)
    l_sc[...] = jnp.zeros(l_sc.shape, F32)
    acc_sc[...] = jnp.zeros(acc_sc.shape, F32)
    qpos = qstart + lax.broadcasted_iota(jnp.int32, (1, n), 1) % tq

    def scores(j, buf):
        ks = pl.multiple_of(j * tk, tk)
        buf[...] = _dot(k_ref[0, 0, pl.ds(ks, tk), :], aug_sc[...])

    def consume(j, buf, masked):
        ks = pl.multiple_of(j * tk, tk)
        st = buf[...]
        if masked:
            st = jnp.where(ks + lax.broadcasted_iota(jnp.int32, (tk, 1), 0) <= qpos, st, NEG)
        _flash_step(st, v_ref[0, 0, :, pl.ds(ks, tk)], m_sc, l_sc, acc_sc)

    _pipelined_sweep(qstart // tk, scores, consume, sa_sc, sb_sc)
    ot = acc_sc[...] / l_sc[...]
    o_ref[0] = jnp.concatenate([ot[:, r * tq:(r + 1) * tq] for r in range(R)], axis=0).T


def _nsa_slc(qt, k, vt, mneg, tq=256, tk=512):
    B, G, R, dh, S = qt.shape
    nsb = mneg.shape[2]
    ka = k.shape[-1]
    assert ka == _slc_aug_width(nsb)
    tq = min(tq, S)
    tk = min(tk, S)
    assert tk % tq == 0, "one masked diagonal key chunk per query tile"
    return pl.pallas_call(
        functools.partial(_slc_kernel, tq=tq, tk=tk, nsb=nsb),
        grid=(B, G, S // tq),
        in_specs=[pl.BlockSpec((1, 1, R, dh, tq), lambda b, g, i: (b, g, 0, 0, i)),
                  pl.BlockSpec((1, 1, S, ka), lambda b, g, i: (b, g, 0, 0)),
                  pl.BlockSpec((1, 1, dh, S), lambda b, g, i: (b, g, 0, 0)),
                  pl.BlockSpec((1, 1, nsb, tq), lambda b, g, i: (b, g, 0, i))],
        out_specs=pl.BlockSpec((1, tq, R * dh), lambda b, g, i: (b, i, g)),
        out_shape=jax.ShapeDtypeStruct((B, S, G * R * dh), F32),
        scratch_shapes=[pltpu.VMEM((ka, R * tq), BF16), pltpu.VMEM((1, R * tq), F32),
                        pltpu.VMEM((1, R * tq), F32), pltpu.VMEM((dh, R * tq), F32),
                        pltpu.VMEM((tk, R * tq), F32), pltpu.VMEM((tk, R * tq), F32)],
        compiler_params=_params(("parallel", "parallel", "arbitrary")),
        name="nsa_slc",
    )(qt, k, vt, mneg)


def _win_kernel(q_ref, k_ref, v_ref, o_ref, *, tq):
    g = pl.program_id(1)
    qi = pl.program_id(2)
    qstart = qi * tq
    R = NSA_REP
    span = tq + NSA_WINDOW
    q = q_ref[0, 0].reshape(R * tq, NSA_DIM)
    ks = pl.multiple_of(qstart, tq)
    s_all = _dot_nt(q, k_ref[0, 0, pl.ds(ks, span), :])
    kpos = qstart - NSA_WINDOW + lax.broadcasted_iota(jnp.int32, (1, span), 1)
    qpos = qstart + lax.broadcasted_iota(jnp.int32, (tq, 1), 0)
    dist = qpos - kpos
    valid = (dist >= 0) & (dist < NSA_WINDOW) & (kpos >= 0)
    distf = dist.astype(F32)
    parts = []
    for r in range(R):
        s = s_all[r * tq:(r + 1) * tq] - _nsa_slope(g, jnp.int32(r)) * distf
        s = jnp.where(valid, s, NEG)
        e = jnp.exp2(s - jnp.max(s, axis=-1, keepdims=True))
        parts.append(e / jnp.sum(e, axis=-1, keepdims=True))
    p = jnp.concatenate(parts, axis=0).astype(BF16)
    o_ref[0, 0] = _dot(p, v_ref[0, 0, pl.ds(ks, span), :]).reshape(R, tq, NSA_DIM)


def _nsa_win(q, kpad, vpad, tq=256):
    B, G, R, S, dh = q.shape
    SP = kpad.shape[2]
    tq = min(tq, S)
    return pl.pallas_call(
        functools.partial(_win_kernel, tq=tq),
        grid=(B, G, S // tq),
        in_specs=[pl.BlockSpec((1, 1, R, tq, dh), lambda b, g, i: (b, g, 0, i, 0)),
                  pl.BlockSpec((1, 1, SP, dh), lambda b, g, i: (b, g, 0, 0)),
                  pl.BlockSpec((1, 1, SP, dh), lambda b, g, i: (b, g, 0, 0))],
        out_specs=pl.BlockSpec((1, 1, R, tq, dh), lambda b, g, i: (b, g, 0, i, 0)),
        out_shape=jax.ShapeDtypeStruct((B, G, R, S, dh), F32),
        compiler_params=_params(("parallel", "parallel", "parallel")),
        name="nsa_win",
    )(q, kpad, vpad)


def _mix_out_kernel(x_ref, oa_ref, oc_ref, os_ref, ow_ref, pg_ref, wa_ref, wb_ref, wo_ref, h_ref):
    D = D_MODEL
    nq = NSA_Q
    gl = 2 * D
    o_b = (_sigmoid(pg_ref[:, gl:gl + nq]) * oc_ref[...]
           + _sigmoid(pg_ref[:, gl + nq:gl + 2 * nq]) * os_ref[...]
           + _sigmoid(pg_ref[:, gl + 2 * nq:gl + 3 * nq]) * ow_ref[...])
    merged = (_sigmoid(pg_ref[:, 0:D]) * _dot(oa_ref[...], wa_ref[...])
              + _sigmoid(pg_ref[:, D:2 * D]) * _dot(o_b.astype(BF16), wb_ref[...]))
    h_ref[...] = x_ref[...] + _dot(merged.astype(BF16), wo_ref[...])


def _mix_out(x2, oa, oc, osl, ow, pg, wa, wb, wo, tm=256):
    T, D = x2.shape
    row = lambda i: (i, 0)
    const = lambda i: (0, 0)
    return pl.pallas_call(
        _mix_out_kernel,
        grid=(T // tm,),
        in_specs=[pl.BlockSpec((tm, D), row), pl.BlockSpec((tm, DA_V), row),
                  pl.BlockSpec((tm, NSA_Q), row), pl.BlockSpec((tm, NSA_Q), row),
                  pl.BlockSpec((tm, NSA_Q), row), pl.BlockSpec((tm, NG_COLS), row),
                  pl.BlockSpec((DA_V, D), const), pl.BlockSpec((NSA_Q, D), const),
                  pl.BlockSpec((D, D), const)],
        out_specs=pl.BlockSpec((tm, D), row),
        out_shape=jax.ShapeDtypeStruct((T, D), F32),
        compiler_params=_params(("parallel",)),
        name="mix_out",
    )(x2, oa, oc, osl, ow, pg, wa, wb, wo)


def _mem_kv_kernel(m_ref, g_ref, w_ref, o_ref):
    o_ref[...] = _dot(_rms(m_ref[...], g_ref[...]).astype(BF16), w_ref[...]).astype(o_ref.dtype)


def _mem_kv(mem2, g, w):
    M, D = mem2.shape
    N = w.shape[1]
    return pl.pallas_call(
        _mem_kv_kernel,
        grid=(1,),
        in_specs=[pl.BlockSpec((M, D), lambda i: (0, 0)), pl.BlockSpec((1, D), lambda i: (0, 0)),
                  pl.BlockSpec((D, N), lambda i: (0, 0))],
        out_specs=pl.BlockSpec((M, N), lambda i: (0, 0)),
        out_shape=jax.ShapeDtypeStruct((M, N), BF16),
        compiler_params=_params(("arbitrary",)),
        name="mem_kv",
    )(mem2, g, w)


def _cross_kernel(h_ref, g_ref, wq_ref, kv_ref, wo_ref, o_ref):
    h = h_ref[0]
    u = _rms(h, g_ref[...]).astype(BF16)
    q = (_dot(u, wq_ref[...]) * np.float32(CA_DIM ** -0.5)).astype(BF16)
    hd = CA_HEADS * CA_DIM
    outs = []
    for a in range(CA_HEADS):
        kh = kv_ref[0, :, a * CA_DIM:(a + 1) * CA_DIM]
        vh = kv_ref[0, :, hd + a * CA_DIM:hd + (a + 1) * CA_DIM]
        s = _dot_nt(q[:, a * CA_DIM:(a + 1) * CA_DIM], kh)
        e = jnp.exp(s - jnp.max(s, axis=-1, keepdims=True))
        p = e / jnp.sum(e, axis=-1, keepdims=True)
        outs.append(_dot(p.astype(BF16), vh))
    o = jnp.concatenate(outs, axis=-1).astype(BF16)
    o_ref[0] = h + _dot(o, wo_ref[...])


def _cross_attn(h, g, wq, kv, wo, tm=256):
    B, S, D = h.shape
    M = kv.shape[1]
    hd = CA_HEADS * CA_DIM
    return pl.pallas_call(
        _cross_kernel,
        grid=(B, S // tm),
        in_specs=[pl.BlockSpec((1, tm, D), lambda b, i: (b, i, 0)),
                  pl.BlockSpec((1, D), lambda b, i: (0, 0)),
                  pl.BlockSpec((D, hd), lambda b, i: (0, 0)),
                  pl.BlockSpec((1, M, 2 * hd), lambda b, i: (b, 0, 0)),
                  pl.BlockSpec((hd, D), lambda b, i: (0, 0))],
        out_specs=pl.BlockSpec((1, tm, D), lambda b, i: (b, i, 0)),
        out_shape=jax.ShapeDtypeStruct((B, S, D), F32),
        compiler_params=_params(("parallel", "parallel")),
        name="cross_attn",
    )(h, g, wq, kv, wo)


def _peer_query_kernel(h_ref, g_ref, wqt_ref, sk_ref, u_ref, s_ref):
    ut = _rms(h_ref[...], g_ref[...]).T.astype(BF16)
    u_ref[...] = ut
    half = PEER_DKEY // 2
    qt = _dot(wqt_ref[...], ut).astype(BF16)
    for hc in range(2 * PEER_HEADS):
        s_ref[hc] = _dot(sk_ref[hc], qt[hc * half:(hc + 1) * half])


def _peer_query(h2, g, wqt, sk, tm=512):
    T, D = h2.shape
    HC = sk.shape[0]
    tm = min(tm, T)
    return pl.pallas_call(
        _peer_query_kernel,
        grid=(T // tm,),
        in_specs=[pl.BlockSpec((tm, D), lambda i: (i, 0)), pl.BlockSpec((1, D), lambda i: (0, 0)),
                  pl.BlockSpec(wqt.shape, lambda i: (0, 0)),
                  pl.BlockSpec(sk.shape, lambda i: (0, 0, 0))],
        out_specs=[pl.BlockSpec((D, tm), lambda i: (0, i)),
                   pl.BlockSpec((HC, PEER_NKEYS, tm), lambda i: (0, 0, i))],
        out_shape=[jax.ShapeDtypeStruct((D, T), BF16),
                   jax.ShapeDtypeStruct((HC, PEER_NKEYS, T), F32)],
        compiler_params=_params(("parallel",)),
        name="peer_query",
    )(h2, g, wqt, sk)


def _top_rows(x, k):
    n = x.shape[0]
    row = lax.broadcasted_iota(jnp.int32, x.shape, 0)
    rank = jnp.full(x.shape, float(k), F32)
    vals = []
    for i in range(k):
        mx = jnp.max(x, axis=0, keepdims=True)
        idx = jnp.min(jnp.where(x == mx, row, n), axis=0, keepdims=True)
        hit = row == idx
        rank = jnp.where(hit, float(i), rank)
        x = jnp.where(hit, -jnp.inf, x)
        vals.append(mx)
    return vals, rank


def _peer_route_kernel(s_ref, cnt_ref, a_ref, rank_ref, b_ref):
    K = PEER_TOPK
    s0 = s_ref[0]
    s1 = s_ref[1]
    v0, rank0 = _top_rows(s0, K)
    v1, rank1 = _top_rows(s1, K)
    top0 = jnp.concatenate(v0, axis=0)
    top1 = jnp.concatenate(v1, axis=0)
    sub = lax.broadcasted_iota(jnp.int32, (8, 1), 0)
    pieces = [v0[0] + top1, v0[1] + top1[0:8]]
    for k0 in range(2, 8):
        pieces.append(jnp.where(sub < K // (k0 + 1), v0[k0] + top1[0:8], -jnp.inf))
    pieces.append(top0[8:16] + v1[0])
    cand = jnp.concatenate(pieces, axis=0)
    cvals, crank = _top_rows(cand, K)
    taken = (crank < K).astype(F32)
    z = jnp.exp(cvals[0] - cvals[0])
    for i in range(1, K):
        z = z + jnp.exp(cvals[i] - cvals[0])
    cnt0 = jnp.zeros(s0.shape, F32)
    for k0 in range(8):
        lo = 0 if k0 == 0 else 8 + 8 * k0
        hi = 16 if k0 == 0 else lo + 8
        cnt = jnp.sum(taken[lo:hi], axis=0, keepdims=True)
        cnt0 = jnp.where(rank0 == float(k0), cnt, cnt0)
    for k0 in range(8, K):
        cnt0 = jnp.where(rank0 == float(k0), taken[72 + k0 - 8:73 + k0 - 8], cnt0)
    cnt_ref[0] = cnt0
    a_ref[0] = jnp.where(rank0 < K, 0.5 * jnp.exp(s0 - v0[0]) / z, 0.0)
    rank_ref[0] = rank1.astype(rank_ref.dtype)
    b_ref[0] = jnp.where(rank1 < K, jnp.exp(s1 - v1[0]), 0.0).astype(b_ref.dtype)


def _peer_route(st, tl=256):
    HC, NK, T = st.shape
    H = HC // 2
    tl = min(tl, T)
    spec = pl.BlockSpec((1, NK, tl), lambda h, i: (h, 0, i))
    f32 = jax.ShapeDtypeStruct((H, NK, T), F32)
    bf16 = jax.ShapeDtypeStruct((H, NK, T), BF16)
    return pl.pallas_call(
        _peer_route_kernel,
        grid=(H, T // tl),
        in_specs=[pl.BlockSpec((2, NK, tl), lambda h, i: (h, 0, i))],
        out_specs=[spec, spec, spec, spec],
        out_shape=[f32, f32, bf16, bf16],
        compiler_params=_params(("parallel", "parallel")),
        name="peer_route",
    )(st)


def _peer_expert_kernel(u_ref, ut_ref, vt_ref, cnt_ref, a_ref, rank_ref, b_ref, o_ref, gate_sc, *, rows):
    e = pl.program_id(1)
    nk = PEER_NKEYS
    tt = u_ref.shape[1]

    @pl.when(e == 0)
    def _():
        o_ref[...] = jnp.zeros(o_ref.shape, F32)

    for ii in range(rows):
        gate = jnp.zeros((nk, tt), BF16)
        for h in range(PEER_HEADS):
            cnt = jnp.broadcast_to(cnt_ref[ii, h:h + 1, :], (nk, tt)).astype(BF16)
            a = jnp.broadcast_to(a_ref[ii, h:h + 1, :], (nk, tt)).astype(BF16)
            gate = gate + jnp.where(rank_ref[h] < cnt, b_ref[h] * a, 0.0)
        gate_sc[ii * nk:(ii + 1) * nk, :] = gate
    blk = 2 * nk
    parts = []
    for r0 in range(0, rows * nk, blk):
        x = _dot(ut_ref[r0:r0 + blk, :], u_ref[...]).astype(BF16)
        parts.append(gate_sc[r0:r0 + blk, :] * (x * (1.0 + lax.erf(x * float(1.0 / np.sqrt(2.0))))))
    o_ref[...] += _dot(vt_ref[...], jnp.concatenate(parts, axis=0))


def _peer_expert(u3t, ut, vt, cnt, a, rank, b, tt=512, rows=8):
    D, T = u3t.shape
    NE = ut.shape[0]
    H, NK, _ = rank.shape
    te = rows * NK
    tt = min(tt, T)
    ispec = pl.BlockSpec((rows, H, tt), lambda t, e: (e, 0, t))
    jspec = pl.BlockSpec((H, NK, tt), lambda t, e: (0, 0, t))
    return pl.pallas_call(
        functools.partial(_peer_expert_kernel, rows=rows),
        grid=(T // tt, NE // te),
        in_specs=[pl.BlockSpec((D, tt), lambda t, e: (0, t)),
                  pl.BlockSpec((te, D), lambda t, e: (e, 0)),
                  pl.BlockSpec((D, te), lambda t, e: (0, e)),
                  ispec, ispec, jspec, jspec],
        out_specs=pl.BlockSpec((D, tt), lambda t, e: (0, t)),
        out_shape=jax.ShapeDtypeStruct((D, T), F32),
        scratch_shapes=[pltpu.VMEM((te, tt), BF16)],
        compiler_params=_params(("parallel", "arbitrary")),
        name="peer_expert",
    )(u3t, ut, vt, cnt, a, rank, b)


def _final_kernel(h_ref, pt_ref, g_ref, o_ref):
    o_ref[...] = _rms(h_ref[...] + pt_ref[...].T, g_ref[...])


def _final_norm(h2, pt, g, tm=256):
    T, D = h2.shape
    return pl.pallas_call(
        _final_kernel,
        grid=(T // tm,),
        in_specs=[pl.BlockSpec((tm, D), lambda i: (i, 0)), pl.BlockSpec((D, tm), lambda i: (0, i)),
                  pl.BlockSpec((1, D), lambda i: (0, 0))],
        out_specs=pl.BlockSpec((tm, D), lambda i: (i, 0)),
        out_shape=jax.ShapeDtypeStruct((T, D), F32),
        compiler_params=_params(("parallel",)),
        name="final_norm",
    )(h2, pt, g)


def _regroup_w_in(w):
    qa, ka, va, qb, kvb, gb, ga, gbm = jnp.split(w, np.cumsum(IN_SIZES)[:-1].tolist(), axis=-1)
    gbe = gb.reshape(-1, NSA_HEADS, 3).transpose(0, 2, 1)
    gbe = jnp.repeat(gbe, NSA_DIM, axis=-1)
    gbe = gbe.reshape(w.shape[0], 3 * NSA_Q)
    return jnp.concatenate([qa * (DA_DIM ** -0.5 * LOG2E), ka, va, qb * (NSA_DIM ** -0.5 * LOG2E), kvb,
                            ga, gbm, gbe], axis=-1).astype(BF16)


def _layer(h, mem, l, p):
    B, S, D = h.shape
    T = B * S
    G, R, dh = NSA_GROUPS, NSA_REP, NSA_DIM
    lambda_init = 0.8 - 0.6 * float(np.exp(-0.3 * l))
    x2 = h.reshape(T, D)

    pb, pg = _in_proj(x2, p["norm_mix"][l][None], _regroup_w_in(p["w_in"][l]))
    pb = pb.reshape(B, S, NB_COLS)
    c = 0
    qa = pb[..., c:c + DA_QK].reshape(B, S, DA_HEADS, 2, DA_DIM).transpose(0, 2, 3, 4, 1); c += DA_QK
    ka = pb[..., c:c + DA_QK].reshape(B, S, DA_HEADS, 2, DA_DIM).transpose(0, 2, 3, 1, 4); c += DA_QK
    va = pb[..., c:c + DA_V].reshape(B, S, DA_HEADS, 2 * DA_DIM).transpose(0, 2, 3, 1); c += DA_V
    qb5 = pb[..., c:c + NSA_Q].reshape(B, S, G, R, dh); c += NSA_Q
    qb = qb5.transpose(0, 2, 3, 1, 4)
    qbt = qb5.transpose(0, 2, 3, 4, 1)
    kvb = pb[..., c:c + NSA_KV].reshape(B, S, 6, G, dh).transpose(2, 0, 3, 1, 4)

    kp = jnp.concatenate([_kpos_cols(S)] * 3 + [jnp.zeros((S, 10), BF16)], axis=-1)
    da_slope = jnp.exp2(-2.0 * jnp.arange(1, DA_HEADS + 1, dtype=F32)) * LOG2E
    q_rows = jnp.stack([c for c in _split3(da_slope) for _ in range(2)]
                       + [jnp.zeros((DA_HEADS,), BF16)] * 10, axis=-1)
    q_rows = jnp.broadcast_to(q_rows[None, :, None, :, None], (B, DA_HEADS, 2, 16, S))
    k_cols = jnp.broadcast_to(kp, (B, DA_HEADS, 2, S, 16))
    o_a = _diff_attn(jnp.concatenate([qa, q_rows], axis=3), jnp.concatenate([ka, k_cols], axis=-1), va,
                     p["diff_lambda"][l], p["diff_subln"][l][:, None], lambda_init)

    nc = S // NSA_CMP_BLOCK
    kdim = NSA_CMP_BLOCK * dh
    xkv = kvb[0:2].reshape(2, B * G * nc, kdim)
    w1 = jnp.stack([p["nsa_ck_w1"][l], p["nsa_cv_w1"][l]]).astype(BF16)
    w2 = jnp.stack([p["nsa_ck_w2"][l], p["nsa_cv_w2"][l]]).astype(BF16)
    kvc = _compress(xkv, p["nsa_cmp_pos"][l].reshape(1, kdim), w1, w2)
    kvc = kvc.reshape(2, B, G, nc // 2, 2, dh).transpose(0, 1, 2, 4, 3, 5).reshape(2, B, G, nc, dh)
    o_cmp, mneg = _nsa_cmp(qbt, kvc[0], kvc[1].transpose(0, 1, 3, 2))
    nsb = S // NSA_SEL_BLOCK
    onehot = (jnp.arange(S, dtype=jnp.int32)[:, None] // NSA_SEL_BLOCK
              == jnp.arange(nsb, dtype=jnp.int32)[None, :]).astype(BF16)
    k_aug = jnp.concatenate([onehot, kp, jnp.zeros((S, _slc_aug_width(nsb) - dh - nsb - 16), BF16)], axis=-1)
    k_aug = jnp.concatenate([kvb[2], jnp.broadcast_to(k_aug, (B, G) + k_aug.shape)], axis=-1)
    o_slc = _nsa_slc(qbt, k_aug, kvb[3].transpose(0, 1, 3, 2), mneg)
    pad = ((0, 0), (0, 0), (NSA_WINDOW, 0), (0, 0))
    o_win = _nsa_win(qb, jnp.pad(kvb[4], pad), jnp.pad(kvb[5], pad))
    o_win = o_win.transpose(0, 3, 1, 2, 4).reshape(T, NSA_Q)

    h1 = _mix_out(x2, o_a.reshape(T, DA_V), o_cmp.reshape(T, NSA_Q), o_slc.reshape(T, NSA_Q), o_win, pg,
                  p["w_branch_a"][l].astype(BF16), p["w_branch_b"][l].astype(BF16),
                  p["w_out"][l].astype(BF16))

    M = mem.shape[1]
    kvm = _mem_kv(mem.reshape(B * M, D), p["norm_mem"][l][None], p["w_cross_kv"][l].astype(BF16))
    h2 = _cross_attn(h1.reshape(B, S, D), p["norm_cross"][l][None], p["w_cross_q"][l].astype(BF16),
                     kvm.reshape(B, M, -1), p["w_cross_o"][l].astype(BF16)).reshape(T, D)

    sk = p["peer_subkeys"][l].reshape(2 * PEER_HEADS, PEER_NKEYS, PEER_DKEY // 2).astype(BF16)
    u3, st = _peer_query(h2, p["norm_ffn"][l][None], p["peer_wq"][l].T.astype(BF16), sk)
    cnt, a, rank, b = _peer_route(st)
    pt = _peer_expert(u3, p["peer_u"][l].astype(BF16), p["peer_v"][l].T.astype(BF16),
                      cnt.transpose(1, 0, 2), a.transpose(1, 0, 2), rank, b)
    return h2, pt


def kernel(x, mem, norm_mix, w_in, diff_lambda, diff_subln, nsa_cmp_pos, nsa_ck_w1, nsa_ck_w2,
           nsa_cv_w1, nsa_cv_w2, w_branch_a, w_branch_b, w_out, norm_cross, norm_mem,
           w_cross_q, w_cross_kv, w_cross_o, norm_ffn, peer_wq, peer_subkeys, peer_u, peer_v,
           norm_final):
    p = dict(norm_mix=norm_mix, w_in=w_in, diff_lambda=diff_lambda, diff_subln=diff_subln,
             nsa_cmp_pos=nsa_cmp_pos, nsa_ck_w1=nsa_ck_w1, nsa_ck_w2=nsa_ck_w2, nsa_cv_w1=nsa_cv_w1,
             nsa_cv_w2=nsa_cv_w2, w_branch_a=w_branch_a, w_branch_b=w_branch_b, w_out=w_out,
             norm_cross=norm_cross, norm_mem=norm_mem, w_cross_q=w_cross_q, w_cross_kv=w_cross_kv,
             w_cross_o=w_cross_o, norm_ffn=norm_ffn, peer_wq=peer_wq, peer_subkeys=peer_subkeys,
             peer_u=peer_u, peer_v=peer_v)
    B, S, D = x.shape
    depth = w_in.shape[0]
    assert depth == 1, "the last layer's PEER output is folded into the final norm"
    h2, pt = _layer(x, mem, 0, p)
    return _final_norm(h2, pt, norm_final[None]).reshape(B, S, D)
```
